```python
import jax
import jax.numpy as jnp
from jax import lax
import numpy as np

D_MODEL = 2048
BATCH = 4
SEQ = 4096
DEPTH = 4

HEAD_DIM = 64
ROPE_THETA = 10000.0
NORM_EPS = 1e-6
MOBA_HEADS = 8
MOBA_W = MOBA_HEADS * HEAD_DIM
MOBA_BLOCK = 256
MOBA_TOPK = 3
MOBA_QBLOCK = 32
SSD_HEADS = 8
SSD_HEAD_DIM = 64
SSD_W = SSD_HEADS * SSD_HEAD_DIM
SSD_GROUPS = 2
SSD_STATE = 128
SSD_CONV = 4
SSD_CHUNK = 256
SSD_CONV_CH = SSD_W + 2 * SSD_GROUPS * SSD_STATE
SWA_HEADS = 8
SWA_KV_HEADS = 2
SWA_W = SWA_HEADS * HEAD_DIM
SWA_KV_W = SWA_KV_HEADS * HEAD_DIM
SWA_WINDOW = 128
S5_W = 512
S5_GROUP = 16
S5_GROUPS = S5_W // S5_GROUP
S5_STATE = 64

MIX_W = MOBA_W + SSD_W + SWA_W + S5_W
IN_SPLITS = (MOBA_W, MOBA_W, MOBA_W, MOBA_W,
             SSD_CONV_CH, SSD_HEADS, SSD_W,
             SWA_W, SWA_KV_W, SWA_KV_W, SWA_W,
             S5_W, S5_W)
IN_W = sum(IN_SPLITS)

kernel_name = 'hybrid_parallel_moba_ssd_swa_s5'

F32 = jnp.float32


def rms_norm(x, g):
    xf = x.astype(F32)
    y = xf * lax.rsqrt(jnp.mean(xf * xf, axis=-1, keepdims=True) + NORM_EPS) * g.astype(F32)
    return y.astype(x.dtype)


def rope(x):
    L, D = x.shape[1], x.shape[-1]
    inv = 1.0 / (ROPE_THETA ** (jnp.arange(0, D, 2, dtype=F32) / D))
    ang = jnp.arange(L, dtype=F32)[:, None] * inv[None, :]
    cos = jnp.cos(ang)[None, :, None, :]
    sin = jnp.sin(ang)[None, :, None, :]
    xf = x.astype(F32)
    x1, x2 = xf[..., : D // 2], xf[..., D // 2:]
    return jnp.concatenate([x1 * cos - x2 * sin, x2 * cos + x1 * sin], axis=-1).astype(x.dtype)


def pad_seq(a, mult):
    pad = (-a.shape[1]) % mult
    return jnp.pad(a, [(0, 0), (0, pad)] + [(0, 0)] * (a.ndim - 2))


def moba_attention(q, k, v):
    B, L, H, D = q.shape
    q, k, v = pad_seq(q, MOBA_BLOCK), pad_seq(k, MOBA_BLOCK), pad_seq(v, MOBA_BLOCK)
    Lp = q.shape[1]
    nb = Lp // MOBA_BLOCK
    nq = Lp // MOBA_QBLOCK
    topk = min(MOBA_TOPK, nb)
    scale = D ** -0.5
    kb = k.reshape(B, nb, MOBA_BLOCK, H, D).transpose(0, 3, 1, 2, 4)
    vb = v.reshape(B, nb, MOBA_BLOCK, H, D).transpose(0, 3, 1, 2, 4)
    k_mean = jnp.mean(kb.astype(F32), axis=3)
    qh = q.transpose(0, 2, 1, 3)
    gate = jnp.einsum('bhld,bhnd->bhln', qh.astype(F32), k_mean)
    q_blk = jnp.arange(Lp) // MOBA_BLOCK
    past = jnp.arange(nb)[None, :] < q_blk[:, None]
    gate = jnp.where(past, gate, -jnp.inf)
    _, sel = lax.top_k(gate, topk)
    valid = sel < q_blk[:, None]

    def to_chunks(t):
        t = t.reshape((B, H, nq, MOBA_QBLOCK) + t.shape[3:])
        return jnp.moveaxis(t, 2, 0)

    b_idx = jnp.arange(B)[:, None, None, None]
    h_idx = jnp.arange(H)[None, :, None, None]

    def query_block(args):
        qc, selc, validc, ci = args
        kg = kb[b_idx, h_idx, selc]
        vg = vb[b_idx, h_idx, selc]
        s_sel = jnp.einsum('bhqd,bhqkjd->bhqkj', qc, kg).astype(F32) * scale
        s_sel = jnp.where(validc[..., None], s_sel, -jnp.inf)
        s_sel = s_sel.reshape(B, H, MOBA_QBLOCK, topk * MOBA_BLOCK)
        own = (ci * MOBA_QBLOCK) // MOBA_BLOCK
        ko = lax.dynamic_index_in_dim(kb, own, axis=2, keepdims=False)
        vo = lax.dynamic_index_in_dim(vb, own, axis=2, keepdims=False)
        s_own = jnp.einsum('bhqd,bhjd->bhqj', qc, ko).astype(F32) * scale
        qpos = ci * MOBA_QBLOCK + jnp.arange(MOBA_QBLOCK)
        kpos = own * MOBA_BLOCK + jnp.arange(MOBA_BLOCK)
        s_own = jnp.where(kpos[None, :] <= qpos[:, None], s_own, -jnp.inf)
        p = jax.nn.softmax(jnp.concatenate([s_sel, s_own], axis=-1), axis=-1).astype(qc.dtype)
        p_sel = p[..., : topk * MOBA_BLOCK].reshape(B, H, MOBA_QBLOCK, topk, MOBA_BLOCK)
        p_own = p[..., topk * MOBA_BLOCK:]
        return (jnp.einsum('bhqkj,bhqkjd->bhqd', p_sel, vg)
                + jnp.einsum('bhqj,bhjd->bhqd', p_own, vo))

    out = lax.map(query_block, (to_chunks(qh), to_chunks(sel), to_chunks(valid),
                                jnp.arange(nq, dtype=jnp.int32)))
    out = jnp.moveaxis(out, 0, 2).reshape(B, H, Lp, D).transpose(0, 2, 1, 3)
    return out[:, :L]


def causal_depthwise_conv(x, w, b):
    K, C = w.shape
    y = lax.conv_general_dilated(x, w[:, None, :].astype(x.dtype), window_strides=(1,),
                                 padding=[(K - 1, 0)],
                                 dimension_numbers=('NWC', 'WIO', 'NWC'),
                                 feature_group_count=C)
    return y + b.astype(x.dtype)


def segsum(a):
    T = a.shape[-1]
    cs = jnp.cumsum(a, axis=-1)
    seg = cs[..., :, None] - cs[..., None, :]
    mask = jnp.tril(jnp.ones((T, T), dtype=bool))
    return jnp.where(mask, seg, -jnp.inf)


def ssd_scan(x, dt, A, Bm, Cm):
    b, l, h, p = x.shape
    n = Bm.shape[-1]
    s = SSD_CHUNK
    c = l // s
    X = (x * dt[..., None]).reshape(b, c, s, h, p)
    a = (dt * A).reshape(b, c, s, h).transpose(0, 3, 1, 2)
    Bc = Bm.reshape(b, c, s, h, n)
    Cc = Cm.reshape(b, c, s, h, n)
    a_cum = jnp.cumsum(a, axis=-1)
    cb = jnp.einsum('bclhn,bcshn->bhcls', Cc, Bc)
    y_diag = jnp.einsum('bhcls,bcshp->bclhp', cb * jnp.exp(segsum(a)), X)
    decay_states = jnp.exp(a_cum[..., -1:] - a_cum).transpose(0, 2, 3, 1)
    chunk_states = jnp.einsum('bclhn,bclhp->bchpn', Bc, X * decay_states[..., None])
    chunk_decay = jnp.exp(a_cum[..., -1])

    def pass_state(state, inp):
        st, dec = inp
        return dec[..., None, None] * state + st, state

    _, prev = lax.scan(pass_state, jnp.zeros((b, h, p, n), F32),
                       (jnp.moveaxis(chunk_states, 1, 0), jnp.moveaxis(chunk_decay, 2, 0)))
    prev = jnp.moveaxis(prev, 0, 1)
    y_off = jnp.einsum('bclhn,bchpn->bclhp', Cc, prev) * jnp.exp(a_cum).transpose(0, 2, 3, 1)[..., None]
    return (y_diag + y_off).reshape(b, l, h, p)


def mamba2_mixer(xbc, dt_raw, z, conv_w, conv_b, dt_bias, a_log, d_skip, norm_w):
    B, L, _ = xbc.shape
    xbc = jax.nn.silu(causal_depthwise_conv(xbc, conv_w, conv_b))
    xs, bm, cm = jnp.split(xbc, [SSD_W, SSD_W + SSD_GROUPS * SSD_STATE], axis=-1)
    rep = SSD_HEADS // SSD_GROUPS
    xs = xs.reshape(B, L, SSD_HEADS, SSD_HEAD_DIM).astype(F32)
    bm = jnp.repeat(bm.reshape(B, L, SSD_GROUPS, SSD_STATE).astype(F32), rep, axis=2)
    cm = jnp.repeat(cm.reshape(B, L, SSD_GROUPS, SSD_STATE).astype(F32), rep, axis=2)
    dt = jax.nn.softplus(dt_raw.astype(F32) + dt_bias.astype(F32))
    A = -jnp.exp(a_log.astype(F32))
    y = ssd_scan(pad_seq(xs, SSD_CHUNK), pad_seq(dt, SSD_CHUNK), A,
                 pad_seq(bm, SSD_CHUNK), pad_seq(cm, SSD_CHUNK))[:, :L]
    y = y + d_skip.astype(F32)[:, None] * xs
    y = y.reshape(B, L, SSD_W) * jax.nn.silu(z.astype(F32))
    yg = y.reshape(B, L, SSD_GROUPS, SSD_W // SSD_GROUPS)
    yg = yg * lax.rsqrt(jnp.mean(yg * yg, axis=-1, keepdims=True) + NORM_EPS)
    return (yg.reshape(B, L, SSD_W) * norm_w.astype(F32)).astype(z.dtype)


def swa_attention(q, k, v, sinks):
    B, L, HQ, D = q.shape
    HKV = k.shape[2]
    G = HQ // HKV
    W = SWA_WINDOW
    nb = L // W
    scale = D ** -0.5
    qb = q.reshape(B, nb, W, HKV, G, D)

    def band(t):
        tb = t.reshape(B, nb, W, HKV, D)
        prev = jnp.pad(tb, ((0, 0), (1, 0), (0, 0), (0, 0), (0, 0)))[:, :-1]
        return jnp.concatenate([prev, tb], axis=2)

    kk, vv = band(k), band(v)
    s = jnp.einsum('bnqhgd,bnkhd->bnhgqk', qb, kk).astype(F32) * scale
    qpos = jnp.arange(nb)[:, None, None] * W + jnp.arange(W)[None, :, None]
    kpos = jnp.arange(nb)[:, None, None] * W - W + jnp.arange(2 * W)[None, None, :]
    diff = qpos - kpos
    mask = (diff >= 0) & (diff < W) & (kpos >= 0)
    s = jnp.where(mask[None, :, None, None], s, -jnp.inf)
    sink = jnp.broadcast_to(sinks.astype(F32).reshape(1, 1, HKV, G, 1, 1), s.shape[:-1] + (1,))
    p = jax.nn.softmax(jnp.concatenate([s, sink], axis=-1), axis=-1)[..., :-1].astype(q.dtype)
    return jnp.einsum('bnhgqk,bnkhd->bnqhgd', p, vv).reshape(B, L, HQ, D)


def s5_mixer(u, a_re, a_im, log_dt, b_re, b_im, c_re, c_im, d_skip, glu_w, glu_b):
    B, L, _ = u.shape
    lam = lax.complex(a_re.astype(F32), a_im.astype(F32))
    step = jnp.exp(log_dt.astype(F32))[:, None]
    a_bar = jnp.exp(lam * step)
    b_bar = ((a_bar - 1.0) / lam)[..., None] * lax.complex(b_re.astype(F32), b_im.astype(F32))
    ug = u.astype(F32).reshape(B, L, S5_GROUPS, S5_GROUP)
    bu = jnp.einsum('blgh,gph->blgp', ug.astype(jnp.complex64), b_bar)

    def combine(e1, e2):
        a1, b1 = e1
        a2, b2 = e2
        return a1 * a2, a2 * b1 + b2

    _, states = lax.associative_scan(combine, (jnp.broadcast_to(a_bar, bu.shape), bu), axis=1)
    c = lax.complex(c_re.astype(F32), c_im.astype(F32))
    y = jnp.real(jnp.einsum('blgp,ghp->blgh', states, c)).reshape(B, L, S5_W)
    y = y + d_skip.astype(F32) * u.astype(F32)
    y = jax.nn.gelu(y)
    y = y * jax.nn.sigmoid(y @ glu_w.astype(F32) + glu_b.astype(F32))
    return y.astype(u.dtype)


def hybrid_layer(x, pre_g, post_g, w_in, w_out, conv_w, conv_b, dt_bias, a_log, ssd_d, ssd_norm,
                 sinks, a_re, a_im, log_dt, b_re, b_im, c_re, c_im, s5_d, glu_w, glu_b):
    B, L, _ = x.shape
    h = rms_norm(x, pre_g)
    proj = h @ w_in
    split_points = [int(v) for v in np.cumsum(IN_SPLITS)[:-1]]
    (mq, mk, mv, mg, xbc, dt_raw, z, sq, sk, sv, sg, su, s5g) = jnp.split(proj, split_points, axis=-1)

    def heads(t):
        return t.reshape(B, L, -1, HEAD_DIM)

    y_moba = moba_attention(rope(heads(mq)), rope(heads(mk)), heads(mv)).reshape(B, L, MOBA_W)
    y_moba = y_moba * jax.nn.silu(mg)
    y_ssd = mamba2_mixer(xbc, dt_raw, z, conv_w, conv_b, dt_bias, a_log, ssd_d, ssd_norm)
    y_swa = swa_attention(rope(heads(sq)), rope(heads(sk)), heads(sv), sinks).reshape(B, L, SWA_W)
    y_swa = y_swa * jax.nn.silu(sg)
    y_s5 = s5_mixer(su, a_re, a_im, log_dt, b_re, b_im, c_re, c_im, s5_d, glu_w, glu_b)
    y_s5 = y_s5 * jax.nn.silu(s5g)
    mix = jnp.concatenate([y_moba, y_ssd.astype(x.dtype), y_swa, y_s5], axis=-1)
    return x + rms_norm(mix @ w_out, post_g)


def setup_inputs(seed: int = 0) -> dict:
    key = jax.random.key(seed)
    ks = jax.random.split(key, 24)

    def nrm(k, shape, scale):
        return scale * jax.random.normal(k, shape, F32)

    x = nrm(ks[0], (BATCH, SEQ, D_MODEL), 1.0)
    pre_norm = 1.0 + nrm(ks[1], (DEPTH, D_MODEL), 0.05)
    post_norm = 1.0 + nrm(ks[2], (DEPTH, D_MODEL), 0.05)
    w_in = nrm(ks[3], (DEPTH, D_MODEL, IN_W), D_MODEL ** -0.5)
    w_out = nrm(ks[4], (DEPTH, MIX_W, D_MODEL), MIX_W ** -0.5)
    ssd_conv_w = nrm(ks[5], (DEPTH, SSD_CONV, SSD_CONV_CH), SSD_CONV ** -0.5)
    ssd_conv_b = nrm(ks[6], (DEPTH, SSD_CONV_CH), 0.02)
    dt0 = jnp.exp(jax.random.uniform(ks[7], (DEPTH, SSD_HEADS), F32,
                                     minval=float(np.log(1e-3)), maxval=float(np.log(1e-1))))
    ssd_dt_bias = dt0 + jnp.log(-jnp.expm1(-dt0))
    ssd_a_log = jnp.log(jax.random.uniform(ks[8], (DEPTH, SSD_HEADS), F32, minval=1.0, maxval=16.0))
    ssd_d = 1.0 + nrm(ks[9], (DEPTH, SSD_HEADS), 0.05)
    ssd_norm = 1.0 + nrm(ks[10], (DEPTH, SSD_W), 0.05)
    swa_sinks = nrm(ks[11], (DEPTH, SWA_HEADS), 1.0)
    n_idx = jnp.arange(S5_STATE, dtype=F32)
    s5_a_re = -0.5 + nrm(ks[12], (DEPTH, S5_GROUPS, S5_STATE), 0.01)
    s5_a_im = jnp.pi * n_idx + nrm(ks[13], (DEPTH, S5_GROUPS, S5_STATE), 0.01)
    s5_log_dt = jax.random.uniform(ks[14], (DEPTH, S5_GROUPS), F32,
                                   minval=float(np.log(1e-3)), maxval=float(np.log(1e-1)))
    s5_b_re = nrm(ks[15], (DEPTH, S5_GROUPS, S5_STATE, S5_GROUP), (2 * S5_GROUP) ** -0.5)
    s5_b_im = nrm(ks[16], (DEPTH, S5_GROUPS, S5_STATE, S5_GROUP), (2 * S5_GROUP) ** -0.5)
    s5_c_re = nrm(ks[17], (DEPTH, S5_GROUPS, S5_GROUP, S5_STATE), S5_STATE ** -0.5)
    s5_c_im = nrm(ks[18], (DEPTH, S5_GROUPS, S5_GROUP, S5_STATE), S5_STATE ** -0.5)
    s5_d = nrm(ks[19], (DEPTH, S5_W), 1.0)
    s5_glu_w = nrm(ks[20], (DEPTH, S5_W, S5_W), S5_W ** -0.5)
    s5_glu_b = nrm(ks[21], (DEPTH, S5_W), 0.02)
    return {'x': x, 'pre_norm': pre_norm, 'post_norm': post_norm, 'w_in': w_in, 'w_out': w_out,
            'ssd_conv_w': ssd_conv_w, 'ssd_conv_b': ssd_conv_b, 'ssd_dt_bias': ssd_dt_bias,
            'ssd_a_log': ssd_a_log, 'ssd_d': ssd_d, 'ssd_norm': ssd_norm, 'swa_sinks': swa_sinks,
            's5_a_re': s5_a_re, 's5_a_im': s5_a_im, 's5_log_dt': s5_log_dt,
            's5_b_re': s5_b_re, 's5_b_im': s5_b_im, 's5_c_re': s5_c_re, 's5_c_im': s5_c_im,
            's5_d': s5_d, 's5_glu_w': s5_glu_w, 's5_glu_b': s5_glu_b}


def reference(x, pre_norm, post_norm, w_in, w_out, ssd_conv_w, ssd_conv_b, ssd_dt_bias, ssd_a_log,
              ssd_d, ssd_norm, swa_sinks, s5_a_re, s5_a_im, s5_log_dt, s5_b_re, s5_b_im,
              s5_c_re, s5_c_im, s5_d, s5_glu_w, s5_glu_b):
    for l in range(DEPTH):
        x = hybrid_layer(x, pre_norm[l], post_norm[l], w_in[l], w_out[l],
                         ssd_conv_w[l], ssd_conv_b[l], ssd_dt_bias[l], ssd_a_log[l], ssd_d[l], ssd_norm[l],
                         swa_sinks[l], s5_a_re[l], s5_a_im[l], s5_log_dt[l], s5_b_re[l], s5_b_im[l],
                         s5_c_re[l], s5_c_im[l], s5_d[l], s5_glu_w[l], s5_glu_b[l])
    return x
```

```python
import functools
import math

import jax
import jax.numpy as jnp
import numpy as np
from jax import lax
from jax.experimental import pallas as pl
from jax.experimental.pallas import tpu as pltpu

F32 = jnp.float32
BF16 = jnp.bfloat16

D_MODEL = 2048
HEAD_DIM = 64
ROPE_THETA = 10000.0
NORM_EPS = 1e-6
MOBA_HEADS = 8
MOBA_W = MOBA_HEADS * HEAD_DIM
MOBA_BLOCK = 256
MOBA_TOPK = 3
SSD_HEADS = 8
SSD_HEAD_DIM = 64
SSD_W = SSD_HEADS * SSD_HEAD_DIM
SSD_GROUPS = 2
SSD_STATE = 128
SSD_CONV = 4
SSD_CHUNK = 256
SSD_CONV_CH = SSD_W + 2 * SSD_GROUPS * SSD_STATE
SWA_HEADS = 8
SWA_KV_HEADS = 2
SWA_W = SWA_HEADS * HEAD_DIM
SWA_KV_W = SWA_KV_HEADS * HEAD_DIM
SWA_WINDOW = 128
S5_W = 512
S5_GROUP = 16
S5_GROUPS = S5_W // S5_GROUP
S5_STATE = 64
MIX_W = MOBA_W + SSD_W + SWA_W + S5_W

LANES = 128
SUBLANES = 8
VMEM_LIMIT_BYTES = 48 * 1024 * 1024

PROJ_TILE = 512
T_MQ, T_MK, T_MV, T_MG, T_XBC, T_Z, T_SQ, T_SG, T_SU, T_S5G, T_TAIL = 0, 1, 2, 3, 4, 6, 7, 8, 9, 10, 11
PROJ_W = 12 * PROJ_TILE
TAIL_K_BLK = T_TAIL * (PROJ_TILE // LANES)
TAIL_V_BLK = TAIL_K_BLK + 1
TAIL_DT_BLK = TAIL_K_BLK + 2

NEG = -1e30
S5_CHUNK = 32
S5_ROWS = 8


def _silu(x):
    return x / (1.0 + jnp.exp(-x))


def _sigmoid(x):
    return 1.0 / (1.0 + jnp.exp(-x))


def _nt_dot(a, b, precision=None):
    return lax.dot_general(a, b, (((1,), (1,)), ((), ())), precision=precision,
                           preferred_element_type=F32)


def _tn_dot(a, b, precision=None):
    return lax.dot_general(a, b, (((0,), (0,)), ((), ())), precision=precision,
                           preferred_element_type=F32)


def _inproj_kernel(x_ref, g_ref, w_ref, cos_ref, sin_ref, o_ref, h_scr):
    j = pl.program_id(1)

    @pl.when(j == 0)
    def _():
        x = x_ref[...]
        ms = jnp.mean(x * x, axis=-1, keepdims=True)
        h_scr[...] = (x * lax.rsqrt(ms + NORM_EPS) * g_ref[...]).astype(BF16)

    acc = jnp.dot(h_scr[...], w_ref[...], preferred_element_type=F32)
    is_rope = (j == T_MQ) | (j == T_MK) | (j == T_SQ) | (j == T_TAIL)

    @pl.when(is_rope)
    def _():
        n = acc.shape[1]
        lane = lax.broadcasted_iota(jnp.int32, acc.shape, 1)
        half = HEAD_DIM // 2
        first = (lane % HEAD_DIM) < half
        rot = jnp.where(first, pltpu.roll(acc, n - half, axis=1), pltpu.roll(acc, half, axis=1))
        roped = acc * cos_ref[...] + rot * sin_ref[...]
        apply = (j != T_TAIL) | (lane < SWA_KV_W)
        o_ref[...] = jnp.where(apply, roped, acc)

    @pl.when(jnp.logical_not(is_rope))
    def _():
        o_ref[...] = acc


def _inproj(x2, pre_g, w_cat, cos_t, sin_t, seq, tm):
    m = x2.shape[0]
    nseq = seq // tm
    return pl.pallas_call(
        _inproj_kernel,
        grid=(m // tm, PROJ_W // PROJ_TILE),
        in_specs=[
            pl.BlockSpec((tm, D_MODEL), lambda i, j: (i, 0)),
            pl.BlockSpec((1, D_MODEL), lambda i, j: (0, 0)),
            pl.BlockSpec((D_MODEL, PROJ_TILE), lambda i, j: (0, j)),
            pl.BlockSpec((tm, PROJ_TILE), lambda i, j: (i % nseq, 0)),
            pl.BlockSpec((tm, PROJ_TILE), lambda i, j: (i % nseq, 0)),
        ],
        out_specs=pl.BlockSpec((tm, PROJ_TILE), lambda i, j: (i, j)),
        out_shape=jax.ShapeDtypeStruct((m, PROJ_W), F32),
        scratch_shapes=[pltpu.VMEM((tm, D_MODEL), BF16)],
        compiler_params=pltpu.CompilerParams(
            dimension_semantics=("parallel", "arbitrary"),
            vmem_limit_bytes=VMEM_LIMIT_BYTES),
        name="inproj",
    )(x2, pre_g, w_cat, cos_t, sin_t)


def _moba_kernel(q_ref, k_ref, v_ref, g_ref, o_ref, kaug_scr, v_scr, kmean_scr, *, nb):
    qi = pl.program_id(2)
    blk = MOBA_BLOCK
    scale = HEAD_DIM ** -0.5

    @pl.when(qi == 0)
    def _():
        k = k_ref[...]
        seq = k.shape[0]
        kaug_scr[:, :LANES] = k.astype(BF16)
        row_blk = lax.broadcasted_iota(jnp.int32, (seq, LANES), 0) // blk
        lane = lax.broadcasted_iota(jnp.int32, (seq, LANES), 1)
        kaug_scr[:, LANES:] = jnp.where(row_blk == lane, 1.0, 0.0).astype(BF16)
        v_scr[...] = v_ref[...].astype(BF16)
        kmean_scr[...] = jnp.mean(k.reshape(nb, blk, LANES), axis=1)

    q = q_ref[...]
    lane = lax.broadcasted_iota(jnp.int32, q.shape, 1)
    row = lax.broadcasted_iota(jnp.int32, (blk, blk), 0)
    col = lax.broadcasted_iota(jnp.int32, (blk, blk), 1)
    causal = col <= row
    blk_id = lax.broadcasted_iota(jnp.int32, (nb, blk), 0)
    past = blk_id < qi
    own_start = pl.multiple_of(qi * blk, blk)

    outs = []
    for hh in range(2):
        qh = jnp.where((lane // HEAD_DIM) == hh, q, 0.0)
        gate = _nt_dot(kmean_scr[...], qh, precision=lax.Precision.HIGHEST)
        gm = jnp.where(past, gate, -jnp.inf)
        cnt = jnp.zeros((nb, blk), F32)
        for n in range(nb):
            gn = gm[n:n + 1, :]
            beats = (gn > gm) | ((gn == gm) & (blk_id > n))
            cnt = cnt + jnp.where(beats & (qi > n), 1.0, 0.0)
        keep = (past & (cnt < float(MOBA_TOPK))) | (blk_id == qi)
        pen_t = jnp.where(keep, 0.0, NEG)
        pen_t = jnp.concatenate([pen_t, jnp.zeros((LANES - nb, blk), F32)], axis=0)
        pen = pen_t.T
        q_aug = jnp.concatenate([(qh * scale).astype(BF16), pen.astype(BF16)], axis=1)

        s = _nt_dot(q_aug, kaug_scr[pl.ds(own_start, blk), :])
        s = jnp.where(causal, s, NEG)
        m0 = jnp.max(s, axis=-1, keepdims=True)
        p = jnp.exp(s - m0)
        l0 = jnp.sum(p, axis=-1, keepdims=True)
        acc0 = jnp.dot(p.astype(BF16), v_scr[pl.ds(own_start, blk), :],
                       preferred_element_type=F32)

        def body(n, carry):
            m, l, acc = carry
            start = pl.multiple_of(n * blk, blk)
            s = _nt_dot(q_aug, kaug_scr[pl.ds(start, blk), :])
            m_new = jnp.maximum(m, jnp.max(s, axis=-1, keepdims=True))
            alpha = jnp.exp(m - m_new)
            p = jnp.exp(s - m_new)
            l = alpha * l + jnp.sum(p, axis=-1, keepdims=True)
            acc = alpha * acc + jnp.dot(p.astype(BF16), v_scr[pl.ds(start, blk), :],
                                        preferred_element_type=F32)
            return m_new, l, acc

        _, l, acc = lax.fori_loop(0, qi, body, (m0, l0, acc0))
        outs.append(acc / l)

    out = jnp.where(lane < HEAD_DIM, outs[0], outs[1])
    o_ref[...] = (out * _silu(g_ref[...])).astype(o_ref.dtype)


def _moba(proj3):
    b, seq, _ = proj3.shape
    blk = MOBA_BLOCK
    nb = seq // blk
    npair = MOBA_W // LANES
    per = PROJ_TILE // LANES
    return pl.pallas_call(
        functools.partial(_moba_kernel, nb=nb),
        grid=(b, npair, nb),
        in_specs=[
            pl.BlockSpec((None, blk, LANES), lambda bi, hp, qi: (bi, qi, T_MQ * per + hp)),
            pl.BlockSpec((None, seq, LANES), lambda bi, hp, qi: (bi, 0, T_MK * per + hp)),
            pl.BlockSpec((None, seq, LANES), lambda bi, hp, qi: (bi, 0, T_MV * per + hp)),
            pl.BlockSpec((None, blk, LANES), lambda bi, hp, qi: (bi, qi, T_MG * per + hp)),
        ],
        out_specs=pl.BlockSpec((None, blk, LANES), lambda bi, hp, qi: (bi, qi, hp)),
        out_shape=jax.ShapeDtypeStruct((b, seq, MOBA_W), BF16),
        scratch_shapes=[pltpu.VMEM((seq, 2 * LANES), BF16),
                        pltpu.VMEM((seq, LANES), BF16),
                        pltpu.VMEM((nb, LANES), F32)],
        compiler_params=pltpu.CompilerParams(
            dimension_semantics=("parallel", "parallel", "arbitrary"),
            vmem_limit_bytes=VMEM_LIMIT_BYTES),
        name="moba",
    )(proj3, proj3, proj3, proj3)


def _swa_kernel(sink_ref, q_ref, kp_ref, kc_ref, vp_ref, vc_ref, g_ref, o_ref):
    n = pl.program_id(1)
    w = SWA_WINDOW
    scale = HEAD_DIM ** -0.5
    kcat = jnp.concatenate([kp_ref[...], kc_ref[...]], axis=0).astype(BF16)
    vcat = jnp.concatenate([vp_ref[...], vc_ref[...]], axis=0).astype(BF16)
    row = lax.broadcasted_iota(jnp.int32, (w, 2 * w), 0)
    col = lax.broadcasted_iota(jnp.int32, (w, 2 * w), 1)
    valid = (col > row) & (col <= row + w) & ((n > 0) | (col >= w))
    lane = lax.broadcasted_iota(jnp.int32, (w, LANES), 1)
    g = g_ref[...]
    for i in range(SWA_W // LANES):
        q_t = q_ref[:, i * LANES:(i + 1) * LANES]
        halves = []
        for c in range(2):
            qm = (jnp.where((lane // HEAD_DIM) == c, q_t, 0.0) * scale).astype(BF16)
            s = _nt_dot(qm, kcat)
            s = jnp.where(valid, s, NEG)
            sink = sink_ref[2 * i + c]
            m = jnp.maximum(jnp.max(s, axis=-1, keepdims=True), sink)
            e = jnp.exp(s - m)
            denom = jnp.sum(e, axis=-1, keepdims=True) + jnp.exp(sink - m)
            o = jnp.dot(e.astype(BF16), vcat, preferred_element_type=F32)
            halves.append(o / denom)
        out = jnp.where(lane < HEAD_DIM, halves[0], halves[1])
        g_t = g[:, i * LANES:(i + 1) * LANES]
        o_ref[:, i * LANES:(i + 1) * LANES] = (out * _silu(g_t)).astype(o_ref.dtype)


def _swa(proj3, sinks_perm):
    b, seq, _ = proj3.shape
    w = SWA_WINDOW
    prev = lambda bi, n: (bi, jnp.maximum(n - 1, 0), TAIL_K_BLK)
    prev_v = lambda bi, n: (bi, jnp.maximum(n - 1, 0), TAIL_V_BLK)
    return pl.pallas_call(
        _swa_kernel,
        grid=(b, seq // w),
        in_specs=[
            pl.BlockSpec(memory_space=pltpu.SMEM),
            pl.BlockSpec((None, w, SWA_W), lambda bi, n: (bi, n, T_SQ)),
            pl.BlockSpec((None, w, LANES), prev),
            pl.BlockSpec((None, w, LANES), lambda bi, n: (bi, n, TAIL_K_BLK)),
            pl.BlockSpec((None, w, LANES), prev_v),
            pl.BlockSpec((None, w, LANES), lambda bi, n: (bi, n, TAIL_V_BLK)),
            pl.BlockSpec((None, w, SWA_W), lambda bi, n: (bi, n, T_SG)),
        ],
        out_specs=pl.BlockSpec((None, w, SWA_W), lambda bi, n: (bi, n, 0)),
        out_shape=jax.ShapeDtypeStruct((b, seq, SWA_W), BF16),
        compiler_params=pltpu.CompilerParams(
            dimension_semantics=("parallel", "arbitrary"),
            vmem_limit_bytes=VMEM_LIMIT_BYTES),
        name="swa",
    )(sinks_perm, proj3, proj3, proj3, proj3, proj3, proj3)


def _ssd_kernel(xbc_ref, z_ref, dt_ref, cw_ref, cb_ref, dtb_ref, alog_ref, dsk_ref, nw_ref,
                o_ref, xbuf, state):
    c = pl.program_id(1)
    s = SSD_CHUNK
    pad = SUBLANES
    hi = lax.Precision.HIGHEST

    @pl.when(c == 0)
    def _():
        xbuf[0:pad, :] = jnp.zeros((pad, SSD_CONV_CH), F32)
        state[...] = jnp.zeros_like(state)

    @pl.when(c > 0)
    def _():
        xbuf[0:pad, :] = xbuf[s:s + pad, :]

    xbuf[pad:pad + s, :] = xbc_ref[...]
    conv = cb_ref[...]
    for k in range(SSD_CONV):
        off = pad - (SSD_CONV - 1) + k
        conv = conv + cw_ref[k:k + 1, :] * xbuf[off:off + s, :]
    xbc = _silu(conv)
    xs = xbc[:, :SSD_W]
    gw = SSD_GROUPS * SSD_STATE
    bm = xbc[:, SSD_W:SSD_W + gw]
    cm = xbc[:, SSD_W + gw:]

    dtr = dt_ref[...] + dtb_ref[...]
    dt = jnp.maximum(dtr, 0.0) + jnp.log(1.0 + jnp.exp(-jnp.abs(dtr)))
    a = dt * (-jnp.exp(alog_ref[...]))
    row = lax.broadcasted_iota(jnp.int32, (s, s), 0)
    col = lax.broadcasted_iota(jnp.int32, (s, s), 1)
    lower = row >= col
    tri = jnp.where(lower, 1.0, 0.0)
    acum = jnp.dot(tri, a, precision=hi, preferred_element_type=F32)
    acum_t = acum.T
    alast = acum[s - 1:s, :]

    erow = lax.broadcasted_iota(jnp.int32, (LANES, SSD_W), 0)
    ecol = lax.broadcasted_iota(jnp.int32, (LANES, SSD_W), 1)
    expand = jnp.where(erow == ecol // SSD_HEAD_DIM, 1.0, 0.0)
    ex = lambda t: jnp.dot(t, expand, precision=hi, preferred_element_type=F32)
    dt_e = ex(dt)
    dec_e = ex(jnp.exp(alast - acum))
    ea_e = ex(jnp.exp(acum))
    cd_e = ex(jnp.broadcast_to(jnp.exp(alast), (SUBLANES, LANES)))[0:1, :]

    x_dt = xs * dt_e
    x_dec = (x_dt * dec_e).astype(BF16)
    x_dt16 = x_dt.astype(BF16)
    lane = lax.broadcasted_iota(jnp.int32, (s, LANES), 1)
    hpg = SSD_HEADS // SSD_GROUPS
    gwid = hpg * SSD_HEAD_DIM
    y_parts = []
    for g in range(SSD_GROUPS):
        bg = bm[:, g * SSD_STATE:(g + 1) * SSD_STATE].astype(BF16)
        cg = cm[:, g * SSD_STATE:(g + 1) * SSD_STATE].astype(BF16)
        cbm = _nt_dot(cg, bg)
        st = state[:, g * gwid:(g + 1) * gwid]
        y_off = jnp.dot(cg, st.astype(BF16), preferred_element_type=F32) \
            * ea_e[:, g * gwid:(g + 1) * gwid]
        new_st = cd_e[:, g * gwid:(g + 1) * gwid] * st + _tn_dot(bg, x_dec[:, g * gwid:(g + 1) * gwid])
        state[:, g * gwid:(g + 1) * gwid] = new_st
        for i in range(hpg // 2):
            lo = g * gwid + i * LANES
            xp = x_dt16[:, lo:lo + LANES]
            ys = []
            for hh in range(2):
                h = g * hpg + 2 * i + hh
                seg = acum[:, h:h + 1] - acum_t[h:h + 1, :]
                lm = jnp.where(lower, jnp.exp(jnp.minimum(seg, 0.0)), 0.0)
                ys.append(jnp.dot((cbm * lm).astype(BF16), xp, preferred_element_type=F32))
            y_parts.append(jnp.where(lane < SSD_HEAD_DIM, ys[0], ys[1])
                           + y_off[:, i * LANES:(i + 1) * LANES])
    y = jnp.concatenate(y_parts, axis=1)
    y = (y + dsk_ref[...] * xs) * _silu(z_ref[...])
    outs = []
    for g in range(SSD_GROUPS):
        yg = y[:, g * gwid:(g + 1) * gwid]
        outs.append(yg * lax.rsqrt(jnp.mean(yg * yg, axis=-1, keepdims=True) + NORM_EPS))
    o_ref[...] = (jnp.concatenate(outs, axis=1) * nw_ref[...]).astype(o_ref.dtype)


def _ssd(proj3, conv_w, conv_b, dt_bias_p, a_log_p, d_skip_e, norm_w):
    b, seq, _ = proj3.shape
    s = SSD_CHUNK
    full = lambda shape: pl.BlockSpec(shape, lambda bi, c: (0, 0))
    return pl.pallas_call(
        _ssd_kernel,
        grid=(b, seq // s),
        in_specs=[
            pl.BlockSpec((None, s, SSD_CONV_CH), lambda bi, c: (bi, c, T_XBC * PROJ_TILE // SSD_CONV_CH)),
            pl.BlockSpec((None, s, SSD_W), lambda bi, c: (bi, c, T_Z)),
            pl.BlockSpec((None, s, LANES), lambda bi, c: (bi, c, TAIL_DT_BLK)),
            full((SSD_CONV, SSD_CONV_CH)),
            full((1, SSD_CONV_CH)),
            full((1, LANES)),
            full((1, LANES)),
            full((1, SSD_W)),
            full((1, SSD_W)),
        ],
        out_specs=pl.BlockSpec((None, s, SSD_W), lambda bi, c: (bi, c, 0)),
        out_shape=jax.ShapeDtypeStruct((b, seq, SSD_W), BF16),
        scratch_shapes=[pltpu.VMEM((s + 2 * SUBLANES, SSD_CONV_CH), F32),
                        pltpu.VMEM((SSD_STATE, SSD_W), F32)],
        compiler_params=pltpu.CompilerParams(
            dimension_semantics=("parallel", "arbitrary"),
            vmem_limit_bytes=VMEM_LIMIT_BYTES),
        name="ssd",
    )(proj3, proj3, proj3, conv_w, conv_b, dt_bias_p, a_log_p, d_skip_e, norm_w)


def _s5_kernel(u_ref, mt_ref, wt_ref, vt_ref, a1_ref, a2_ref, a2s_ref, y_ref, ee_scr, s_scr, *, nchunk):
    u = u_ref[...]
    ee_scr[...] = jnp.dot(u, wt_ref[...], preferred_element_type=F32)
    a1 = a1_ref[...]
    a2 = a2_ref[...]
    a2s = a2s_ref[...]
    p2 = 2 * S5_STATE

    def body(c, carry):
        st, st_sw = carry
        r = pl.multiple_of(c * S5_ROWS, S5_ROWS)
        s_scr[pl.ds(r, S5_ROWS), :] = st
        e = ee_scr[pl.ds(r, S5_ROWS), :]
        new = a1 * st + a2 * st_sw + e[:, :p2]
        new_sw = a1 * st_sw + a2s * st + e[:, p2:]
        return new, new_sw

    zero = jnp.zeros((S5_ROWS, p2), F32)
    lax.fori_loop(0, nchunk, body, (zero, zero))
    y = jnp.dot(u, mt_ref[...], preferred_element_type=F32)
    y = y + jnp.dot(s_scr[...].astype(BF16), vt_ref[...], preferred_element_type=F32)
    y_ref[...] = y


def _s5_core(u_t, mt, wt2, vt, a1, a2, a2s):
    g, rows, tw = u_t.shape
    p2 = 2 * S5_STATE
    return pl.pallas_call(
        functools.partial(_s5_kernel, nchunk=rows // S5_ROWS),
        grid=(g,),
        in_specs=[
            pl.BlockSpec((None, rows, tw), lambda gi: (gi, 0, 0)),
            pl.BlockSpec((None, tw, tw), lambda gi: (gi, 0, 0)),
            pl.BlockSpec((None, tw, 2 * p2), lambda gi: (gi, 0, 0)),
            pl.BlockSpec((None, p2, tw), lambda gi: (gi, 0, 0)),
            pl.BlockSpec((None, 1, p2), lambda gi: (gi, 0, 0)),
            pl.BlockSpec((None, 1, p2), lambda gi: (gi, 0, 0)),
            pl.BlockSpec((None, 1, p2), lambda gi: (gi, 0, 0)),
        ],
        out_specs=pl.BlockSpec((None, rows, tw), lambda gi: (gi, 0, 0)),
        out_shape=jax.ShapeDtypeStruct((g, rows, tw), F32),
        scratch_shapes=[pltpu.VMEM((rows, 2 * p2), F32), pltpu.VMEM((rows, p2), F32)],
        compiler_params=pltpu.CompilerParams(
            dimension_semantics=("parallel",),
            vmem_limit_bytes=VMEM_LIMIT_BYTES),
        name="s5",
    )(u_t, mt, wt2, vt, a1, a2, a2s)


def _s5_tables(a_re, a_im, log_dt, b_re, b_im, c_re, c_im):
    t = S5_CHUNK
    hi = lax.Precision.HIGHEST
    step = jnp.exp(log_dt)[:, None]
    k = jnp.arange(t + 1, dtype=F32)[None, :, None]
    mag = jnp.exp(k * (a_re * step)[:, None, :])
    ang = k * (a_im * step)[:, None, :]
    pw_re, pw_im = mag * jnp.cos(ang), mag * jnp.sin(ang)
    ab_re, ab_im = pw_re[:, 1], pw_im[:, 1]
    den = a_re * a_re + a_im * a_im
    n_re, n_im = ab_re - 1.0, ab_im
    cf_re = (n_re * a_re + n_im * a_im) / den
    cf_im = (n_im * a_re - n_re * a_im) / den
    bb_re = cf_re[..., None] * b_re - cf_im[..., None] * b_im
    bb_im = cf_re[..., None] * b_im + cf_im[..., None] * b_re
    cp_re = c_re[:, None] * pw_re[:, :t, None, :] - c_im[:, None] * pw_im[:, :t, None, :]
    cp_im = c_re[:, None] * pw_im[:, :t, None, :] + c_im[:, None] * pw_re[:, :t, None, :]
    kern = (jnp.einsum('gshp,gpk->gshk', cp_re, bb_re, precision=hi)
            - jnp.einsum('gshp,gpk->gshk', cp_im, bb_im, precision=hi))
    ii = jnp.arange(t)
    lag = ii[None, :] - ii[:, None]
    toe = jnp.where((lag >= 0)[None, :, :, None, None], kern[:, jnp.clip(lag, 0)], 0.0)
    g = a_re.shape[0]
    mt = toe.transpose(0, 1, 4, 2, 3).reshape(g, t * S5_GROUP, t * S5_GROUP)
    rp_re, rp_im = pw_re[:, t - 1::-1][:, :t], pw_im[:, t - 1::-1][:, :t]
    w_re = rp_re[:, :, None, :] * bb_re.transpose(0, 2, 1)[:, None] - rp_im[:, :, None, :] * bb_im.transpose(0, 2, 1)[:, None]
    w_im = rp_re[:, :, None, :] * bb_im.transpose(0, 2, 1)[:, None] + rp_im[:, :, None, :] * bb_re.transpose(0, 2, 1)[:, None]
    w_re = w_re.reshape(g, t * S5_GROUP, S5_STATE)
    w_im = w_im.reshape(g, t * S5_GROUP, S5_STATE)
    wt2 = jnp.concatenate([w_re, w_im, w_im, w_re], axis=-1)
    q_re, q_im = pw_re[:, 1:], pw_im[:, 1:]
    v_re = c_re[:, None] * q_re[:, :, None, :] - c_im[:, None] * q_im[:, :, None, :]
    v_im = c_re[:, None] * q_im[:, :, None, :] + c_im[:, None] * q_re[:, :, None, :]
    v_re = v_re.transpose(0, 3, 1, 2).reshape(g, S5_STATE, t * S5_GROUP)
    v_im = v_im.transpose(0, 3, 1, 2).reshape(g, S5_STATE, t * S5_GROUP)
    vt = jnp.concatenate([v_re, -v_im], axis=1)
    at_re, at_im = pw_re[:, t], pw_im[:, t]
    a1 = jnp.concatenate([at_re, at_re], axis=-1)[:, None, :]
    a2 = jnp.concatenate([-at_im, at_im], axis=-1)[:, None, :]
    a2s = jnp.concatenate([at_im, -at_im], axis=-1)[:, None, :]
    return mt.astype(BF16), wt2.astype(BF16), vt.astype(BF16), a1, a2, a2s


def _outproj_kernel(ym_ref, ys_ref, yw_ref, y5_ref, u_ref, g5_ref, x_ref, w_ref, pg_ref,
                    d5_ref, gw_ref, gb_ref, o_ref):
    y5 = y5_ref[...] + d5_ref[...] * u_ref[...]
    c0 = math.sqrt(2.0 / math.pi)
    y5 = y5 * (0.5 * (1.0 + jnp.tanh(c0 * (y5 + 0.044715 * (y5 * y5 * y5)))))
    gl = jnp.dot(y5.astype(BF16), gw_ref[...], preferred_element_type=F32) + gb_ref[...]
    y5 = y5 * _sigmoid(gl) * _silu(g5_ref[...])
    parts = [ym_ref[...], ys_ref[...], yw_ref[...], y5.astype(BF16)]
    acc = None
    for i, part in enumerate(parts):
        d = jnp.dot(part, w_ref[i * PROJ_TILE:(i + 1) * PROJ_TILE, :], preferred_element_type=F32)
        acc = d if acc is None else acc + d
    ms = jnp.mean(acc * acc, axis=-1, keepdims=True)
    o_ref[...] = x_ref[...] + acc * lax.rsqrt(ms + NORM_EPS) * pg_ref[...]


def _outproj(y_moba, y_ssd, y_swa, y_s5, proj, x2, w_out, post_g, s5_d, glu_w, glu_b, tm):
    m = x2.shape[0]
    rows = lambda w: pl.BlockSpec((tm, w), lambda i: (i, 0))
    full = lambda shape: pl.BlockSpec(shape, lambda i: (0, 0))
    return pl.pallas_call(
        _outproj_kernel,
        grid=(m // tm,),
        in_specs=[
            rows(MOBA_W), rows(SSD_W), rows(SWA_W), rows(S5_W),
            pl.BlockSpec((tm, PROJ_TILE), lambda i: (i, T_SU)),
            pl.BlockSpec((tm, PROJ_TILE), lambda i: (i, T_S5G)),
            rows(D_MODEL),
            full((MIX_W, D_MODEL)),
            full((1, D_MODEL)),
            full((1, S5_W)),
            full((S5_W, S5_W)),
            full((1, S5_W)),
        ],
        out_specs=rows(D_MODEL),
        out_shape=jax.ShapeDtypeStruct((m, D_MODEL), F32),
        compiler_params=pltpu.CompilerParams(
            dimension_semantics=("parallel",),
            vmem_limit_bytes=VMEM_LIMIT_BYTES),
        name="outproj",
    )(y_moba, y_ssd, y_swa, y_s5, proj, proj, x2, w_out, post_g, s5_d, glu_w, glu_b)


def _swa_head_perm():
    per_kv = SWA_HEADS // SWA_KV_HEADS
    heads = [c * per_kv + i for i in range(per_kv) for c in range(SWA_KV_HEADS)]
    cols = np.concatenate([np.arange(h * HEAD_DIM, (h + 1) * HEAD_DIM) for h in heads])
    return np.asarray(heads), cols


def _rope_tables(seq):
    inv = 1.0 / (ROPE_THETA ** (jnp.arange(0, HEAD_DIM, 2, dtype=F32) / HEAD_DIM))
    ang = jnp.arange(seq, dtype=F32)[:, None] * inv[None, :]
    cos, sin = jnp.cos(ang), jnp.sin(ang)
    cos_h = jnp.concatenate([cos, cos], axis=1)
    sin_h = jnp.concatenate([-sin, sin], axis=1)
    reps = PROJ_TILE // HEAD_DIM
    return jnp.tile(cos_h, (1, reps)), jnp.tile(sin_h, (1, reps))


def _cat_w_in(w_in, swa_cols):
    offs = np.cumsum([0, MOBA_W, MOBA_W, MOBA_W, MOBA_W, SSD_CONV_CH, SSD_HEADS, SSD_W,
                      SWA_W, SWA_KV_W, SWA_KV_W, SWA_W, S5_W, S5_W])
    (o_mq, o_mk, o_mv, o_mg, o_xbc, o_dt, o_z, o_sq, o_sk, o_sv, o_sg, o_su, o_s5g, _) = [int(v) for v in offs]
    d = w_in.shape[0]
    sl = lambda o, w: w_in[:, o:o + w]
    pieces = [sl(o_mq, MOBA_W), sl(o_mk, MOBA_W), sl(o_mv, MOBA_W), sl(o_mg, MOBA_W),
              sl(o_xbc, SSD_CONV_CH), sl(o_z, SSD_W),
              sl(o_sq, SWA_W)[:, swa_cols], sl(o_sg, SWA_W)[:, swa_cols],
              sl(o_su, S5_W), sl(o_s5g, S5_W),
              sl(o_sk, SWA_KV_W), sl(o_sv, SWA_KV_W), sl(o_dt, SSD_HEADS),
              jnp.zeros((d, PROJ_TILE - 2 * SWA_KV_W - SSD_HEADS), w_in.dtype)]
    return jnp.concatenate(pieces, axis=1).astype(BF16)


def _pad_lanes(v):
    return jnp.pad(v, (0, LANES - v.shape[0]))[None, :]


def _layer(x2, b, seq, cos_t, sin_t, pre_g, post_g, w_in, w_out, conv_w, conv_b, dt_bias, a_log,
           ssd_d, ssd_norm, sinks, a_re, a_im, log_dt, b_re, b_im, c_re, c_im, s5_d, glu_w, glu_b):
    heads_perm, swa_cols = _swa_head_perm()
    w_cat = _cat_w_in(w_in, swa_cols)
    w_out_p = jnp.concatenate([w_out[:MOBA_W + SSD_W],
                               w_out[MOBA_W + SSD_W:MOBA_W + SSD_W + SWA_W][swa_cols],
                               w_out[MOBA_W + SSD_W + SWA_W:]], axis=0).astype(BF16)

    tm_in = min(1024, seq)
    proj = _inproj(x2, pre_g[None, :], w_cat, cos_t, sin_t, seq, tm_in)
    proj3 = proj.reshape(b, seq, PROJ_W)

    y_moba = _moba(proj3)
    y_swa = _swa(proj3, sinks[heads_perm])
    y_ssd = _ssd(proj3, conv_w, conv_b[None, :], _pad_lanes(dt_bias), _pad_lanes(a_log),
                 jnp.repeat(ssd_d, SSD_HEAD_DIM)[None, :], ssd_norm[None, :])

    t = S5_CHUNK
    nch = seq // t
    su = proj3[:, :, T_SU * PROJ_TILE:(T_SU + 1) * PROJ_TILE]
    u_t = su.reshape(b, nch, t, S5_GROUPS, S5_GROUP).transpose(3, 1, 0, 2, 4)
    u_t = jnp.pad(u_t, ((0, 0), (0, 0), (0, S5_ROWS - b), (0, 0), (0, 0)))
    u_t = u_t.reshape(S5_GROUPS, nch * S5_ROWS, t * S5_GROUP).astype(BF16)
    tables = _s5_tables(a_re, a_im, log_dt, b_re, b_im, c_re, c_im)
    y_t = _s5_core(u_t, *tables)
    y_s5 = y_t.reshape(S5_GROUPS, nch, S5_ROWS, t, S5_GROUP)[:, :, :b]
    y_s5 = y_s5.transpose(2, 1, 3, 0, 4).reshape(b * seq, S5_W)

    m = b * seq
    return _outproj(y_moba.reshape(m, MOBA_W), y_ssd.reshape(m, SSD_W), y_swa.reshape(m, SWA_W),
                    y_s5, proj, x2, w_out_p, post_g[None, :], s5_d[None, :],
                    glu_w.astype(BF16), glu_b[None, :], min(512, seq))


def kernel(x, pre_norm, post_norm, w_in, w_out, ssd_conv_w, ssd_conv_b, ssd_dt_bias, ssd_a_log, ssd_d, ssd_norm, swa_sinks, s5_a_re, s5_a_im, s5_log_dt, s5_b_re, s5_b_im, s5_c_re, s5_c_im, s5_d, s5_glu_w, s5_glu_b):
    b, seq, d = x.shape
    assert d == D_MODEL and b <= S5_ROWS
    assert seq % MOBA_BLOCK == 0 and seq % SSD_CHUNK == 0 and seq % SWA_WINDOW == 0
    cos_t, sin_t = _rope_tables(seq)
    x2 = x.reshape(b * seq, d)
    for l in range(pre_norm.shape[0]):
        x2 = _layer(x2, b, seq, cos_t, sin_t, pre_norm[l], post_norm[l], w_in[l], w_out[l],
                    ssd_conv_w[l], ssd_conv_b[l], ssd_dt_bias[l], ssd_a_log[l], ssd_d[l], ssd_norm[l],
                    swa_sinks[l], s5_a_re[l], s5_a_im[l], s5_log_dt[l], s5_b_re[l], s5_b_im[l],
                    s5_c_re[l], s5_c_im[l], s5_d[l], s5_glu_w[l], s5_glu_b[l])
    return x2.reshape(b, seq, d)
```

```python
import functools
import math

import jax
import jax.numpy as jnp
import numpy as np
from jax import lax
from jax.experimental import pallas as pl
from jax.experimental.pallas import tpu as pltpu

F32 = jnp.float32
BF16 = jnp.bfloat16

D_MODEL = 2048
HEAD_DIM = 64
ROPE_THETA = 10000.0
NORM_EPS = 1e-6
MOBA_HEADS = 8
MOBA_W = MOBA_HEADS * HEAD_DIM
MOBA_BLOCK = 256
MOBA_TOPK = 3
SSD_HEADS = 8
SSD_HEAD_DIM = 64
SSD_W = SSD_HEADS * SSD_HEAD_DIM
SSD_GROUPS = 2
SSD_STATE = 128
SSD_CONV = 4
SSD_CHUNK = 256
SSD_CONV_CH = SSD_W + 2 * SSD_GROUPS * SSD_STATE
SWA_HEADS = 8
SWA_KV_HEADS = 2
SWA_W = SWA_HEADS * HEAD_DIM
SWA_KV_W = SWA_KV_HEADS * HEAD_DIM
SWA_WINDOW = 128
S5_W = 512
S5_GROUP = 16
S5_GROUPS = S5_W // S5_GROUP
S5_STATE = 64
MIX_W = MOBA_W + SSD_W + SWA_W + S5_W

LANES = 128
SUBLANES = 8
VMEM_LIMIT_BYTES = 48 * 1024 * 1024

PROJ_TILE = 512
T_MQ, T_MK, T_MV, T_MG, T_XBC, T_Z, T_SQ, T_SG, T_SU, T_S5G, T_TAIL = 0, 1, 2, 3, 4, 6, 7, 8, 9, 10, 11
PROJ_W = 12 * PROJ_TILE
TAIL_K_BLK = T_TAIL * (PROJ_TILE // LANES)
TAIL_V_BLK = TAIL_K_BLK + 1
TAIL_DT_BLK = TAIL_K_BLK + 2

NEG = -1e30
MOBA_ONES_ROWS = 16
S5_CHUNK = 8
S5_TILE_GROUPS = LANES // S5_GROUP
S5_SCAN_ROWS = SUBLANES


def _silu(x):
    return x / (1.0 + jnp.exp(-x))


def _sigmoid(x):
    return 1.0 / (1.0 + jnp.exp(-x))


def _nt_dot(a, b, precision=None):
    return lax.dot_general(a, b, (((1,), (1,)), ((), ())), precision=precision,
                           preferred_element_type=F32)


def _tn_dot(a, b, precision=None):
    return lax.dot_general(a, b, (((0,), (0,)), ((), ())), precision=precision,
                           preferred_element_type=F32)


def _inproj_kernel(x_ref, g_ref, w_ref, cos_ref, sin_ref, o_ref, h_scr):
    j = pl.program_id(1)

    @pl.when(j == 0)
    def _():
        x = x_ref[...]
        ms = jnp.mean(x * x, axis=-1, keepdims=True)
        h_scr[...] = (x * lax.rsqrt(ms + NORM_EPS) * g_ref[...]).astype(BF16)

    acc = jnp.dot(h_scr[...], w_ref[...], preferred_element_type=F32)
    is_rope = (j == T_MQ) | (j == T_MK) | (j == T_SQ) | (j == T_TAIL)

    @pl.when(is_rope)
    def _():
        n = acc.shape[1]
        lane = lax.broadcasted_iota(jnp.int32, acc.shape, 1)
        half = HEAD_DIM // 2
        first = (lane % HEAD_DIM) < half
        rot = jnp.where(first, pltpu.roll(acc, n - half, axis=1), pltpu.roll(acc, half, axis=1))
        roped = acc * cos_ref[...] + rot * sin_ref[...]
        apply = (j != T_TAIL) | (lane < SWA_KV_W)
        o_ref[...] = jnp.where(apply, roped, acc)

    @pl.when(jnp.logical_not(is_rope))
    def _():
        o_ref[...] = acc


def _inproj(x2, pre_g, w_cat, cos_t, sin_t, seq, tm):
    m = x2.shape[0]
    nseq = seq // tm
    return pl.pallas_call(
        _inproj_kernel,
        grid=(m // tm, PROJ_W // PROJ_TILE),
        in_specs=[
            pl.BlockSpec((tm, D_MODEL), lambda i, j: (i, 0)),
            pl.BlockSpec((1, D_MODEL), lambda i, j: (0, 0)),
            pl.BlockSpec((D_MODEL, PROJ_TILE), lambda i, j: (0, j)),
            pl.BlockSpec((tm, PROJ_TILE), lambda i, j: (i % nseq, 0)),
            pl.BlockSpec((tm, PROJ_TILE), lambda i, j: (i % nseq, 0)),
        ],
        out_specs=pl.BlockSpec((tm, PROJ_TILE), lambda i, j: (i, j)),
        out_shape=jax.ShapeDtypeStruct((m, PROJ_W), F32),
        scratch_shapes=[pltpu.VMEM((tm, D_MODEL), BF16)],
        compiler_params=pltpu.CompilerParams(
            dimension_semantics=("parallel", "arbitrary"),
            vmem_limit_bytes=VMEM_LIMIT_BYTES),
        name="inproj",
    )(x2, pre_g, w_cat, cos_t, sin_t)


def _moba_kernel(q_ref, k_ref, v_ref, g_ref, o_ref, kaug_scr, vt_scr, kmean_scr, qaug_scr, *, nb):
    qi = pl.program_id(2)
    blk = MOBA_BLOCK
    qscale = (HEAD_DIM ** -0.5) * math.log2(math.e)

    @pl.when(qi == 0)
    def _():
        k = k_ref[...]
        seq = k.shape[0]
        kaug_scr[:, :LANES] = k.astype(BF16)
        row_blk = lax.broadcasted_iota(jnp.int32, (seq, LANES), 0) // blk
        lane = lax.broadcasted_iota(jnp.int32, (seq, LANES), 1)
        kaug_scr[:, LANES:] = jnp.where(row_blk == lane, 1.0, 0.0).astype(BF16)
        kmean_scr[...] = jnp.mean(k.reshape(nb, blk, LANES), axis=1)
        ones = jnp.ones((MOBA_ONES_ROWS, blk), BF16)
        for n in range(nb):
            vt = v_ref[n * blk:(n + 1) * blk, :].T.astype(BF16)
            for hh in range(2):
                vt_scr[hh, n, 0:HEAD_DIM, :] = vt[hh * HEAD_DIM:(hh + 1) * HEAD_DIM, :]
                vt_scr[hh, n, HEAD_DIM:, :] = ones

    q_t = q_ref[...].T
    sub = lax.broadcasted_iota(jnp.int32, q_t.shape, 0)
    klane = lax.broadcasted_iota(jnp.int32, (nb, LANES), 1)
    key = lax.broadcasted_iota(jnp.int32, (blk, blk), 0)
    qry = lax.broadcasted_iota(jnp.int32, (blk, blk), 1)
    causal = key <= qry
    blk_id = lax.broadcasted_iota(jnp.int32, (nb, blk), 0)
    past = blk_id < qi

    for hh in range(2):
        km = jnp.where((klane // HEAD_DIM) == hh, kmean_scr[...], 0.0)
        gate = jnp.dot(km, q_t, precision=lax.Precision.HIGHEST, preferred_element_type=F32)
        gm = jnp.where(past, gate, -jnp.inf)
        cnt = jnp.zeros((nb, blk), F32)
        for n in range(nb):
            gn = gm[n:n + 1, :]
            beats = (gn > gm) | ((gn == gm) & (blk_id > n))
            cnt = cnt + jnp.where(beats & (qi > n), 1.0, 0.0)
        keep = (past & (cnt < float(MOBA_TOPK))) | (blk_id == qi)
        pen = jnp.where(keep, 0.0, NEG)
        pen = jnp.concatenate([pen, jnp.zeros((LANES - nb, blk), F32)], axis=0)
        qh = jnp.where((sub // HEAD_DIM) == hh, q_t, 0.0) * qscale
        qaug_scr[hh] = jnp.concatenate([qh.astype(BF16), pen.astype(BF16)], axis=0)

    def scores(n):
        start = pl.multiple_of(n * blk, blk)
        kb = kaug_scr[pl.ds(start, blk), :]
        return [jnp.dot(kb, qaug_scr[hh], preferred_element_type=F32) for hh in range(2)]

    carry = []
    for hh, s in enumerate(scores(qi)):
        s = jnp.where(causal, s, NEG)
        m0 = jnp.max(s, axis=0, keepdims=True)
        p = jnp.exp2(s - m0).astype(BF16)
        carry += [m0, jnp.dot(vt_scr[hh, qi], p, preferred_element_type=F32)]

    def body(n, carry):
        out = []
        for hh, s in enumerate(scores(n)):
            m, acc = carry[2 * hh], carry[2 * hh + 1]
            m_new = jnp.maximum(m, jnp.max(s, axis=0, keepdims=True))
            alpha = jnp.exp2(m - m_new)
            p = jnp.exp2(s - m_new).astype(BF16)
            out += [m_new, alpha * acc + jnp.dot(vt_scr[hh, n], p, preferred_element_type=F32)]
        return tuple(out)

    carry = lax.fori_loop(0, qi, body, tuple(carry))
    outs = [carry[2 * hh + 1][:HEAD_DIM] / carry[2 * hh + 1][HEAD_DIM:HEAD_DIM + 1] for hh in range(2)]
    out = jnp.concatenate(outs, axis=0).T
    o_ref[...] = (out * _silu(g_ref[...])).astype(o_ref.dtype)


def _moba(proj3):
    b, seq, _ = proj3.shape
    blk = MOBA_BLOCK
    nb = seq // blk
    npair = MOBA_W // LANES
    per = PROJ_TILE // LANES
    return pl.pallas_call(
        functools.partial(_moba_kernel, nb=nb),
        grid=(b, npair, nb),
        in_specs=[
            pl.BlockSpec((None, blk, LANES), lambda bi, hp, qi: (bi, qi, T_MQ * per + hp)),
            pl.BlockSpec((None, seq, LANES), lambda bi, hp, qi: (bi, 0, T_MK * per + hp)),
            pl.BlockSpec((None, seq, LANES), lambda bi, hp, qi: (bi, 0, T_MV * per + hp)),
            pl.BlockSpec((None, blk, LANES), lambda bi, hp, qi: (bi, qi, T_MG * per + hp)),
        ],
        out_specs=pl.BlockSpec((None, blk, LANES), lambda bi, hp, qi: (bi, qi, hp)),
        out_shape=jax.ShapeDtypeStruct((b, seq, MOBA_W), BF16),
        scratch_shapes=[pltpu.VMEM((seq, 2 * LANES), BF16),
                        pltpu.VMEM((2, nb, HEAD_DIM + MOBA_ONES_ROWS, blk), BF16),
                        pltpu.VMEM((nb, LANES), F32),
                        pltpu.VMEM((2, 2 * LANES, blk), BF16)],
        compiler_params=pltpu.CompilerParams(
            dimension_semantics=("parallel", "parallel", "arbitrary"),
            vmem_limit_bytes=VMEM_LIMIT_BYTES),
        name="moba",
    )(proj3, proj3, proj3, proj3)


def _swa_kernel(sink_ref, q_ref, kp_ref, kc_ref, vp_ref, vc_ref, g_ref, o_ref):
    n = pl.program_id(1)
    w = SWA_WINDOW
    scale = HEAD_DIM ** -0.5
    kcat = jnp.concatenate([kp_ref[...], kc_ref[...]], axis=0).astype(BF16)
    vcat = jnp.concatenate([vp_ref[...], vc_ref[...]], axis=0).astype(BF16)
    row = lax.broadcasted_iota(jnp.int32, (w, 2 * w), 0)
    col = lax.broadcasted_iota(jnp.int32, (w, 2 * w), 1)
    valid = (col > row) & (col <= row + w) & ((n > 0) | (col >= w))
    lane = lax.broadcasted_iota(jnp.int32, (w, LANES), 1)
    g = g_ref[...]
    for i in range(SWA_W // LANES):
        q_t = q_ref[:, i * LANES:(i + 1) * LANES]
        halves = []
        for c in range(2):
            qm = (jnp.where((lane // HEAD_DIM) == c, q_t, 0.0) * scale).astype(BF16)
            s = _nt_dot(qm, kcat)
            s = jnp.where(valid, s, NEG)
            sink = sink_ref[2 * i + c]
            m = jnp.maximum(jnp.max(s, axis=-1, keepdims=True), sink)
            e = jnp.exp(s - m)
            denom = jnp.sum(e, axis=-1, keepdims=True) + jnp.exp(sink - m)
            o = jnp.dot(e.astype(BF16), vcat, preferred_element_type=F32)
            halves.append(o / denom)
        out = jnp.where(lane < HEAD_DIM, halves[0], halves[1])
        g_t = g[:, i * LANES:(i + 1) * LANES]
        o_ref[:, i * LANES:(i + 1) * LANES] = (out * _silu(g_t)).astype(o_ref.dtype)


def _swa(proj3, sinks_perm):
    b, seq, _ = proj3.shape
    w = SWA_WINDOW
    prev = lambda bi, n: (bi, jnp.maximum(n - 1, 0), TAIL_K_BLK)
    prev_v = lambda bi, n: (bi, jnp.maximum(n - 1, 0), TAIL_V_BLK)
    return pl.pallas_call(
        _swa_kernel,
        grid=(b, seq // w),
        in_specs=[
            pl.BlockSpec(memory_space=pltpu.SMEM),
            pl.BlockSpec((None, w, SWA_W), lambda bi, n: (bi, n, T_SQ)),
            pl.BlockSpec((None, w, LANES), prev),
            pl.BlockSpec((None, w, LANES), lambda bi, n: (bi, n, TAIL_K_BLK)),
            pl.BlockSpec((None, w, LANES), prev_v),
            pl.BlockSpec((None, w, LANES), lambda bi, n: (bi, n, TAIL_V_BLK)),
            pl.BlockSpec((None, w, SWA_W), lambda bi, n: (bi, n, T_SG)),
        ],
        out_specs=pl.BlockSpec((None, w, SWA_W), lambda bi, n: (bi, n, 0)),
        out_shape=jax.ShapeDtypeStruct((b, seq, SWA_W), BF16),
        compiler_params=pltpu.CompilerParams(
            dimension_semantics=("parallel", "arbitrary"),
            vmem_limit_bytes=VMEM_LIMIT_BYTES),
        name="swa",
    )(sinks_perm, proj3, proj3, proj3, proj3, proj3, proj3)


def _ssd_kernel(xbc_ref, z_ref, dt_ref, cw_ref, cb_ref, dtb_ref, alog_ref, dsk_ref, nw_ref,
                o_ref, xbuf, state):
    c = pl.program_id(1)
    s = SSD_CHUNK
    pad = SUBLANES
    hi = lax.Precision.HIGHEST

    @pl.when(c == 0)
    def _():
        xbuf[0:pad, :] = jnp.zeros((pad, SSD_CONV_CH), F32)
        state[...] = jnp.zeros_like(state)

    @pl.when(c > 0)
    def _():
        xbuf[0:pad, :] = xbuf[s:s + pad, :]

    xbuf[pad:pad + s, :] = xbc_ref[...]
    conv = cb_ref[...]
    for k in range(SSD_CONV):
        off = pad - (SSD_CONV - 1) + k
        conv = conv + cw_ref[k:k + 1, :] * xbuf[off:off + s, :]
    xbc = _silu(conv)
    xs = xbc[:, :SSD_W]
    gw = SSD_GROUPS * SSD_STATE
    bm = xbc[:, SSD_W:SSD_W + gw]
    cm = xbc[:, SSD_W + gw:]

    dtr = dt_ref[...] + dtb_ref[...]
    dt = jnp.maximum(dtr, 0.0) + jnp.log(1.0 + jnp.exp(-jnp.abs(dtr)))
    a = dt * (-jnp.exp(alog_ref[...]))
    row = lax.broadcasted_iota(jnp.int32, (s, s), 0)
    col = lax.broadcasted_iota(jnp.int32, (s, s), 1)
    lower = row >= col
    tri = jnp.where(lower, 1.0, 0.0)
    acum = jnp.dot(tri, a, precision=hi, preferred_element_type=F32)
    acum_t = acum.T
    alast = acum[s - 1:s, :]

    erow = lax.broadcasted_iota(jnp.int32, (LANES, SSD_W), 0)
    ecol = lax.broadcasted_iota(jnp.int32, (LANES, SSD_W), 1)
    expand = jnp.where(erow == ecol // SSD_HEAD_DIM, 1.0, 0.0)
    ex = lambda t: jnp.dot(t, expand, precision=hi, preferred_element_type=F32)
    dt_e = ex(dt)
    dec_e = ex(jnp.exp(alast - acum))
    ea_e = ex(jnp.exp(acum))
    cd_e = ex(jnp.broadcast_to(jnp.exp(alast), (SUBLANES, LANES)))[0:1, :]

    x_dt = xs * dt_e
    x_dec = (x_dt * dec_e).astype(BF16)
    x_dt16 = x_dt.astype(BF16)
    lane = lax.broadcasted_iota(jnp.int32, (s, LANES), 1)
    hpg = SSD_HEADS // SSD_GROUPS
    gwid = hpg * SSD_HEAD_DIM
    y_parts = []
    for g in range(SSD_GROUPS):
        bg = bm[:, g * SSD_STATE:(g + 1) * SSD_STATE].astype(BF16)
        cg = cm[:, g * SSD_STATE:(g + 1) * SSD_STATE].astype(BF16)
        cbm = _nt_dot(cg, bg)
        st = state[:, g * gwid:(g + 1) * gwid]
        y_off = jnp.dot(cg, st.astype(BF16), preferred_element_type=F32) \
            * ea_e[:, g * gwid:(g + 1) * gwid]
        new_st = cd_e[:, g * gwid:(g + 1) * gwid] * st + _tn_dot(bg, x_dec[:, g * gwid:(g + 1) * gwid])
        state[:, g * gwid:(g + 1) * gwid] = new_st
        for i in range(hpg // 2):
            lo = g * gwid + i * LANES
            xp = x_dt16[:, lo:lo + LANES]
            ys = []
            for hh in range(2):
                h = g * hpg + 2 * i + hh
                seg = acum[:, h:h + 1] - acum_t[h:h + 1, :]
                lm = jnp.where(lower, jnp.exp(jnp.minimum(seg, 0.0)), 0.0)
                ys.append(jnp.dot((cbm * lm).astype(BF16), xp, preferred_element_type=F32))
            y_parts.append(jnp.where(lane < SSD_HEAD_DIM, ys[0], ys[1])
                           + y_off[:, i * LANES:(i + 1) * LANES])
    y = jnp.concatenate(y_parts, axis=1)
    y = (y + dsk_ref[...] * xs) * _silu(z_ref[...])
    outs = []
    for g in range(SSD_GROUPS):
        yg = y[:, g * gwid:(g + 1) * gwid]
        outs.append(yg * lax.rsqrt(jnp.mean(yg * yg, axis=-1, keepdims=True) + NORM_EPS))
    o_ref[...] = (jnp.concatenate(outs, axis=1) * nw_ref[...]).astype(o_ref.dtype)


def _ssd(proj3, conv_w, conv_b, dt_bias_p, a_log_p, d_skip_e, norm_w):
    b, seq, _ = proj3.shape
    s = SSD_CHUNK
    full = lambda shape: pl.BlockSpec(shape, lambda bi, c: (0, 0))
    return pl.pallas_call(
        _ssd_kernel,
        grid=(b, seq // s),
        in_specs=[
            pl.BlockSpec((None, s, SSD_CONV_CH), lambda bi, c: (bi, c, T_XBC * PROJ_TILE // SSD_CONV_CH)),
            pl.BlockSpec((None, s, SSD_W), lambda bi, c: (bi, c, T_Z)),
            pl.BlockSpec((None, s, LANES), lambda bi, c: (bi, c, TAIL_DT_BLK)),
            full((SSD_CONV, SSD_CONV_CH)),
            full((1, SSD_CONV_CH)),
            full((1, LANES)),
            full((1, LANES)),
            full((1, SSD_W)),
            full((1, SSD_W)),
        ],
        out_specs=pl.BlockSpec((None, s, SSD_W), lambda bi, c: (bi, c, 0)),
        out_shape=jax.ShapeDtypeStruct((b, seq, SSD_W), BF16),
        scratch_shapes=[pltpu.VMEM((s + 2 * SUBLANES, SSD_CONV_CH), F32),
                        pltpu.VMEM((SSD_STATE, SSD_W), F32)],
        compiler_params=pltpu.CompilerParams(
            dimension_semantics=("parallel", "arbitrary"),
            vmem_limit_bytes=VMEM_LIMIT_BYTES),
        name="ssd",
    )(proj3, proj3, proj3, conv_w, conv_b, dt_bias_p, a_log_p, d_skip_e, norm_w)


def _s5_kernel(u_ref, w_ref, k_ref, v_ref, are_ref, aim_ref, y_ref, x_scr, e_scr, s_scr, *, nrow):
    t = S5_CHUNK
    half = S5_TILE_GROUPS * S5_STATE
    for j in range(t):
        x_scr[j] = u_ref[pl.ds(j, nrow, stride=t), :].astype(BF16)
    e = jnp.dot(x_scr[0], w_ref[0], preferred_element_type=F32)
    for j in range(1, t):
        e = e + jnp.dot(x_scr[j], w_ref[j], preferred_element_type=F32)
    e_scr[...] = e
    a_re = are_ref[...]
    a_im = aim_ref[...]

    def body(it, carry):
        s_re, s_im = carry
        r0 = pl.multiple_of(it * S5_SCAN_ROWS, S5_SCAN_ROWS)
        e8 = e_scr[pl.ds(r0, S5_SCAN_ROWS), :]
        rows = []
        for kk in range(S5_SCAN_ROWS):
            rows.append(jnp.concatenate([s_re, s_im], axis=1))
            e_re, e_im = e8[kk:kk + 1, :half], e8[kk:kk + 1, half:]
            s_re, s_im = a_re * s_re - a_im * s_im + e_re, a_re * s_im + a_im * s_re + e_im
        s_scr[pl.ds(r0, S5_SCAN_ROWS), :] = jnp.concatenate(rows, axis=0)
        return s_re, s_im

    zero = jnp.zeros((1, half), F32)
    lax.fori_loop(0, nrow // S5_SCAN_ROWS, body, (zero, zero))
    s16 = s_scr[...].astype(BF16)
    for i in range(t):
        y = jnp.dot(s16, v_ref[i], preferred_element_type=F32)
        for j in range(i + 1):
            y = y + jnp.dot(x_scr[j], k_ref[i - j], preferred_element_type=F32)
        y_ref[pl.ds(i, nrow, stride=t), :] = y


def _s5_core(proj3, w, k, v, a_re, a_im):
    b, seq, _ = proj3.shape
    t = S5_CHUNK
    nrow = seq // t
    ntile = S5_W // LANES
    half = S5_TILE_GROUPS * S5_STATE
    per = PROJ_TILE // LANES
    return pl.pallas_call(
        functools.partial(_s5_kernel, nrow=nrow),
        grid=(ntile, b),
        in_specs=[
            pl.BlockSpec((None, seq, LANES), lambda q, bi: (bi, 0, T_SU * per + q)),
            pl.BlockSpec((None, t, LANES, 2 * half), lambda q, bi: (q, 0, 0, 0)),
            pl.BlockSpec((None, t, LANES, LANES), lambda q, bi: (q, 0, 0, 0)),
            pl.BlockSpec((None, t, 2 * half, LANES), lambda q, bi: (q, 0, 0, 0)),
            pl.BlockSpec((None, 1, half), lambda q, bi: (q, 0, 0)),
            pl.BlockSpec((None, 1, half), lambda q, bi: (q, 0, 0)),
        ],
        out_specs=pl.BlockSpec((None, seq, LANES), lambda q, bi: (bi, 0, q)),
        out_shape=jax.ShapeDtypeStruct((b, seq, S5_W), F32),
        scratch_shapes=[pltpu.VMEM((t, nrow, LANES), BF16),
                        pltpu.VMEM((nrow, 2 * half), F32),
                        pltpu.VMEM((nrow, 2 * half), F32)],
        compiler_params=pltpu.CompilerParams(
            dimension_semantics=("parallel", "parallel"),
            vmem_limit_bytes=VMEM_LIMIT_BYTES),
        name="s5",
    )(proj3, w, k, v, a_re, a_im)


def _s5_tables(a_re, a_im, log_dt, b_re, b_im, c_re, c_im):
    t = S5_CHUNK
    hi = lax.Precision.HIGHEST
    step = jnp.exp(log_dt)[:, None]
    k = jnp.arange(t + 1, dtype=F32)[None, :, None]
    mag = jnp.exp(k * (a_re * step)[:, None, :])
    ang = k * (a_im * step)[:, None, :]
    pw_re, pw_im = mag * jnp.cos(ang), mag * jnp.sin(ang)
    ab_re, ab_im = pw_re[:, 1], pw_im[:, 1]
    den = a_re * a_re + a_im * a_im
    n_re, n_im = ab_re - 1.0, ab_im
    cf_re = (n_re * a_re + n_im * a_im) / den
    cf_im = (n_im * a_re - n_re * a_im) / den
    bb_re = cf_re[..., None] * b_re - cf_im[..., None] * b_im
    bb_im = cf_re[..., None] * b_im + cf_im[..., None] * b_re
    cp_re = c_re[:, None] * pw_re[:, :t, None, :] - c_im[:, None] * pw_im[:, :t, None, :]
    cp_im = c_re[:, None] * pw_im[:, :t, None, :] + c_im[:, None] * pw_re[:, :t, None, :]
    kern = (jnp.einsum('gshp,gpk->gshk', cp_re, bb_re, precision=hi)
            - jnp.einsum('gshp,gpk->gshk', cp_im, bb_im, precision=hi))
    g = a_re.shape[0]
    tg = S5_TILE_GROUPS
    nt = g // tg
    eye = jnp.eye(tg, dtype=F32)

    def tile_diag(m, pattern):
        m5 = m.reshape(nt, tg, t, m.shape[2], m.shape[3])
        out = jnp.einsum(pattern, m5, eye)
        return out.reshape(nt, t, out.shape[2] * out.shape[3], out.shape[4] * out.shape[5])

    k_bd = tile_diag(kern, 'qgsab,gk->qsgbka')
    k_rev = (t - 1) - jnp.arange(t, dtype=F32)[None, :, None]
    mag_r = jnp.exp(k_rev * (a_re * step)[:, None, :])
    ang_r = k_rev * (a_im * step)[:, None, :]
    rp_re, rp_im = mag_r * jnp.cos(ang_r), mag_r * jnp.sin(ang_r)
    bt_re, bt_im = bb_re.transpose(0, 2, 1)[:, None], bb_im.transpose(0, 2, 1)[:, None]
    w_re = rp_re[:, :, None, :] * bt_re - rp_im[:, :, None, :] * bt_im
    w_im = rp_re[:, :, None, :] * bt_im + rp_im[:, :, None, :] * bt_re
    w_bd = jnp.concatenate([tile_diag(w_re, 'qgjhp,gk->qjghkp'),
                            tile_diag(w_im, 'qgjhp,gk->qjghkp')], axis=-1)
    q_re, q_im = pw_re[:, 1:], pw_im[:, 1:]
    v_re = c_re[:, None] * q_re[:, :, None, :] - c_im[:, None] * q_im[:, :, None, :]
    v_im = c_re[:, None] * q_im[:, :, None, :] + c_im[:, None] * q_re[:, :, None, :]
    v_bd = jnp.concatenate([tile_diag(v_re, 'qgihp,gk->qigpkh'),
                            tile_diag(-v_im, 'qgihp,gk->qigpkh')], axis=2)
    at_re = pw_re[:, t].reshape(nt, 1, tg * S5_STATE)
    at_im = pw_im[:, t].reshape(nt, 1, tg * S5_STATE)
    return w_bd.astype(BF16), k_bd.astype(BF16), v_bd.astype(BF16), at_re, at_im


def _outproj_kernel(ym_ref, ys_ref, yw_ref, y5_ref, u_ref, g5_ref, x_ref, w_ref, pg_ref,
                    d5_ref, gw_ref, gb_ref, o_ref):
    y5 = y5_ref[...] + d5_ref[...] * u_ref[...]
    c0 = math.sqrt(2.0 / math.pi)
    y5 = y5 * (0.5 * (1.0 + jnp.tanh(c0 * (y5 + 0.044715 * (y5 * y5 * y5)))))
    gl = jnp.dot(y5.astype(BF16), gw_ref[...], preferred_element_type=F32) + gb_ref[...]
    y5 = y5 * _sigmoid(gl) * _silu(g5_ref[...])
    parts = [ym_ref[...], ys_ref[...], yw_ref[...], y5.astype(BF16)]
    acc = None
    for i, part in enumerate(parts):
        d = jnp.dot(part, w_ref[i * PROJ_TILE:(i + 1) * PROJ_TILE, :], preferred_element_type=F32)
        acc = d if acc is None else acc + d
    ms = jnp.mean(acc * acc, axis=-1, keepdims=True)
    o_ref[...] = x_ref[...] + acc * lax.rsqrt(ms + NORM_EPS) * pg_ref[...]


def _outproj(y_moba, y_ssd, y_swa, y_s5, proj, x2, w_out, post_g, s5_d, glu_w, glu_b, tm):
    m = x2.shape[0]
    rows = lambda w: pl.BlockSpec((tm, w), lambda i: (i, 0))
    full = lambda shape: pl.BlockSpec(shape, lambda i: (0, 0))
    return pl.pallas_call(
        _outproj_kernel,
        grid=(m // tm,),
        in_specs=[
            rows(MOBA_W), rows(SSD_W), rows(SWA_W), rows(S5_W),
            pl.BlockSpec((tm, PROJ_TILE), lambda i: (i, T_SU)),
            pl.BlockSpec((tm, PROJ_TILE), lambda i: (i, T_S5G)),
            rows(D_MODEL),
            full((MIX_W, D_MODEL)),
            full((1, D_MODEL)),
            full((1, S5_W)),
            full((S5_W, S5_W)),
            full((1, S5_W)),
        ],
        out_specs=rows(D_MODEL),
        out_shape=jax.ShapeDtypeStruct((m, D_MODEL), F32),
        compiler_params=pltpu.CompilerParams(
            dimension_semantics=("parallel",),
            vmem_limit_bytes=VMEM_LIMIT_BYTES),
        name="outproj",
    )(y_moba, y_ssd, y_swa, y_s5, proj, proj, x2, w_out, post_g, s5_d, glu_w, glu_b)


def _swa_head_perm():
    per_kv = SWA_HEADS // SWA_KV_HEADS
    heads = [c * per_kv + i for i in range(per_kv) for c in range(SWA_KV_HEADS)]
    cols = np.concatenate([np.arange(h * HEAD_DIM, (h + 1) * HEAD_DIM) for h in heads])
    return np.asarray(heads), cols


def _rope_tables(seq):
    inv = 1.0 / (ROPE_THETA ** (jnp.arange(0, HEAD_DIM, 2, dtype=F32) / HEAD_DIM))
    ang = jnp.arange(seq, dtype=F32)[:, None] * inv[None, :]
    cos, sin = jnp.cos(ang), jnp.sin(ang)
    cos_h = jnp.concatenate([cos, cos], axis=1)
    sin_h = jnp.concatenate([-sin, sin], axis=1)
    reps = PROJ_TILE // HEAD_DIM
    return jnp.tile(cos_h, (1, reps)), jnp.tile(sin_h, (1, reps))


def _cat_w_in(w_in, swa_cols):
    offs = np.cumsum([0, MOBA_W, MOBA_W, MOBA_W, MOBA_W, SSD_CONV_CH, SSD_HEADS, SSD_W,
                      SWA_W, SWA_KV_W, SWA_KV_W, SWA_W, S5_W, S5_W])
    (o_mq, o_mk, o_mv, o_mg, o_xbc, o_dt, o_z, o_sq, o_sk, o_sv, o_sg, o_su, o_s5g, _) = [int(v) for v in offs]
    d = w_in.shape[0]
    sl = lambda o, w: w_in[:, o:o + w]
    pieces = [sl(o_mq, MOBA_W), sl(o_mk, MOBA_W), sl(o_mv, MOBA_W), sl(o_mg, MOBA_W),
              sl(o_xbc, SSD_CONV_CH), sl(o_z, SSD_W),
              sl(o_sq, SWA_W)[:, swa_cols], sl(o_sg, SWA_W)[:, swa_cols],
              sl(o_su, S5_W), sl(o_s5g, S5_W),
              sl(o_sk, SWA_KV_W), sl(o_sv, SWA_KV_W), sl(o_dt, SSD_HEADS),
              jnp.zeros((d, PROJ_TILE - 2 * SWA_KV_W - SSD_HEADS), w_in.dtype)]
    return jnp.concatenate(pieces, axis=1).astype(BF16)


def _pad_lanes(v):
    return jnp.pad(v, (0, LANES - v.shape[0]))[None, :]


def _layer(x2, b, seq, cos_t, sin_t, pre_g, post_g, w_in, w_out, conv_w, conv_b, dt_bias, a_log,
           ssd_d, ssd_norm, sinks, a_re, a_im, log_dt, b_re, b_im, c_re, c_im, s5_d, glu_w, glu_b):
    heads_perm, swa_cols = _swa_head_perm()
    w_cat = _cat_w_in(w_in, swa_cols)
    w_out_p = jnp.concatenate([w_out[:MOBA_W + SSD_W],
                               w_out[MOBA_W + SSD_W:MOBA_W + SSD_W + SWA_W][swa_cols],
                               w_out[MOBA_W + SSD_W + SWA_W:]], axis=0).astype(BF16)

    tm_in = min(1024, seq)
    proj = _inproj(x2, pre_g[None, :], w_cat, cos_t, sin_t, seq, tm_in)
    proj3 = proj.reshape(b, seq, PROJ_W)

    y_moba = _moba(proj3)
    y_swa = _swa(proj3, sinks[heads_perm])
    y_ssd = _ssd(proj3, conv_w, conv_b[None, :], _pad_lanes(dt_bias), _pad_lanes(a_log),
                 jnp.repeat(ssd_d, SSD_HEAD_DIM)[None, :], ssd_norm[None, :])

    y_s5 = _s5_core(proj3, *_s5_tables(a_re, a_im, log_dt, b_re, b_im, c_re, c_im))
    y_s5 = y_s5.reshape(b * seq, S5_W)

    m = b * seq
    return _outproj(y_moba.reshape(m, MOBA_W), y_ssd.reshape(m, SSD_W), y_swa.reshape(m, SWA_W),
                    y_s5, proj, x2, w_out_p, post_g[None, :], s5_d[None, :],
                    glu_w.astype(BF16), glu_b[None, :], min(512, seq))


def kernel(x, pre_norm, post_norm, w_in, w_out, ssd_conv_w, ssd_conv_b, ssd_dt_bias, ssd_a_log, ssd_d, ssd_norm, swa_sinks, s5_a_re, s5_a_im, s5_log_dt, s5_b_re, s5_b_im, s5_c_re, s5_c_im, s5_d, s5_glu_w, s5_glu_b):
    b, seq, d = x.shape
    assert d == D_MODEL
    assert seq % MOBA_BLOCK == 0 and seq % SSD_CHUNK == 0 and seq % SWA_WINDOW == 0
    assert seq % (S5_CHUNK * S5_SCAN_ROWS) == 0
    cos_t, sin_t = _rope_tables(seq)
    x2 = x.reshape(b * seq, d)
    for l in range(pre_norm.shape[0]):
        x2 = _layer(x2, b, seq, cos_t, sin_t, pre_norm[l], post_norm[l], w_in[l], w_out[l],
                    ssd_conv_w[l], ssd_conv_b[l], ssd_dt_bias[l], ssd_a_log[l], ssd_d[l], ssd_norm[l],
                    swa_sinks[l], s5_a_re[l], s5_a_im[l], s5_log_dt[l], s5_b_re[l], s5_b_im[l],
                    s5_c_re[l], s5_c_im[l], s5_d[l], s5_glu_w[l], s5_glu_b[l])
    return x2.reshape(b, seq, d)
```

```python
import functools
import math

import jax
import jax.numpy as jnp
import numpy as np
from jax import lax
from jax.experimental import pallas as pl
from jax.experimental.pallas import tpu as pltpu

F32 = jnp.float32
BF16 = jnp.bfloat16

D_MODEL = 2048
HEAD_DIM = 64
ROPE_THETA = 10000.0
NORM_EPS = 1e-6
MOBA_HEADS = 8
MOBA_W = MOBA_HEADS * HEAD_DIM
MOBA_BLOCK = 256
MOBA_TOPK = 3
SSD_HEADS = 8
SSD_HEAD_DIM = 64
SSD_W = SSD_HEADS * SSD_HEAD_DIM
SSD_GROUPS = 2
SSD_STATE = 128
SSD_CONV = 4
SSD_CHUNK = 256
SSD_CONV_CH = SSD_W + 2 * SSD_GROUPS * SSD_STATE
SWA_HEADS = 8
SWA_KV_HEADS = 2
SWA_W = SWA_HEADS * HEAD_DIM
SWA_KV_W = SWA_KV_HEADS * HEAD_DIM
SWA_WINDOW = 128
S5_W = 512
S5_GROUP = 16
S5_GROUPS = S5_W // S5_GROUP
S5_STATE = 64
MIX_W = MOBA_W + SSD_W + SWA_W + S5_W

LANES = 128
SUBLANES = 8
VMEM_LIMIT_BYTES = 48 * 1024 * 1024

PROJ_TILE = 512
T_MQ, T_MK, T_MV, T_MG, T_XBC, T_Z, T_SQ, T_SG, T_SU, T_S5G, T_TAIL = 0, 1, 2, 3, 4, 6, 7, 8, 9, 10, 11
PROJ_W = 12 * PROJ_TILE
TAIL_K_BLK = T_TAIL * (PROJ_TILE // LANES)
TAIL_V_BLK = TAIL_K_BLK + 1
TAIL_DT_BLK = TAIL_K_BLK + 2

NEG = -1e30
TM_INPROJ = 1024
TM_OUTPROJ = 512
SWA_TQ = 512
MOBA_ONES_ROWS = 16
S5_CHUNK = 8
S5_TILE_GROUPS = LANES // S5_GROUP
S5_SCAN_ROWS = SUBLANES


def _silu(x):
    return x / (1.0 + jnp.exp(-x))


def _sigmoid(x):
    return 1.0 / (1.0 + jnp.exp(-x))


def _nt_dot(a, b, precision=None):
    return lax.dot_general(a, b, (((1,), (1,)), ((), ())), precision=precision,
                           preferred_element_type=F32)


def _tn_dot(a, b, precision=None):
    return lax.dot_general(a, b, (((0,), (0,)), ((), ())), precision=precision,
                           preferred_element_type=F32)


def _inproj_kernel(x_ref, g_ref, w_ref, cos_ref, sin_ref, o_ref, h_scr):
    j = pl.program_id(1)

    @pl.when(j == 0)
    def _():
        x = x_ref[...]
        ms = jnp.mean(x * x, axis=-1, keepdims=True)
        h_scr[...] = (x * lax.rsqrt(ms + NORM_EPS) * g_ref[...]).astype(BF16)

    acc = jnp.dot(h_scr[...], w_ref[...], preferred_element_type=F32)
    is_rope = (j == T_MQ) | (j == T_MK) | (j == T_SQ) | (j == T_TAIL)

    @pl.when(is_rope)
    def _():
        rot = jnp.concatenate(
            [pltpu.roll(acc[:, t * LANES:(t + 1) * LANES], LANES // 2, axis=1)
             for t in range(acc.shape[1] // LANES)], axis=1)
        roped = acc * cos_ref[...] + rot * sin_ref[...]
        lane = lax.broadcasted_iota(jnp.int32, acc.shape, 1)
        apply = (j != T_TAIL) | (lane < SWA_KV_W)
        o_ref[...] = jnp.where(apply, roped, acc)

    @pl.when(jnp.logical_not(is_rope))
    def _():
        o_ref[...] = acc


def _inproj(x2, pre_g, w_cat, cos_t, sin_t, seq, tm):
    m = x2.shape[0]
    nseq = seq // tm
    return pl.pallas_call(
        _inproj_kernel,
        grid=(m // tm, PROJ_W // PROJ_TILE),
        in_specs=[
            pl.BlockSpec((tm, D_MODEL), lambda i, j: (i, 0)),
            pl.BlockSpec((1, D_MODEL), lambda i, j: (0, 0)),
            pl.BlockSpec((D_MODEL, PROJ_TILE), lambda i, j: (0, j)),
            pl.BlockSpec((tm, PROJ_TILE), lambda i, j: (i % nseq, 0)),
            pl.BlockSpec((tm, PROJ_TILE), lambda i, j: (i % nseq, 0)),
        ],
        out_specs=pl.BlockSpec((tm, PROJ_TILE), lambda i, j: (i, j)),
        out_shape=jax.ShapeDtypeStruct((m, PROJ_W), F32),
        scratch_shapes=[pltpu.VMEM((tm, D_MODEL), BF16)],
        compiler_params=pltpu.CompilerParams(
            dimension_semantics=("parallel", "arbitrary"),
            vmem_limit_bytes=VMEM_LIMIT_BYTES),
        name="inproj",
    )(x2, pre_g, w_cat, cos_t, sin_t)


def _pair_head(idx):
    return (idx // (HEAD_DIM // 2)) % 2


def _moba_kernel(q_ref, k_ref, v_ref, g_ref, o_ref, kaug_scr, vt_scr, kmean_scr, qaug_scr,
                 acc_scr, sa_scr, sb_scr, *, nb):
    qi = pl.program_id(2)
    blk = MOBA_BLOCK
    qscale = (HEAD_DIM ** -0.5) * math.log2(math.e)

    @pl.when(qi == 0)
    def _():
        k = k_ref[...]
        seq = k.shape[0]
        kaug_scr[:, :LANES] = k.astype(BF16)
        row_blk = lax.broadcasted_iota(jnp.int32, (seq, LANES), 0) // blk
        lane = lax.broadcasted_iota(jnp.int32, (seq, LANES), 1)
        kaug_scr[:, LANES:] = jnp.where(row_blk == lane, 1.0, 0.0).astype(BF16)
        kmean_scr[...] = jnp.mean(k.reshape(nb, blk, LANES), axis=1)
        ones = jnp.ones((MOBA_ONES_ROWS, blk), BF16)
        for n in range(nb):
            vt = v_ref[n * blk:(n + 1) * blk, :].T.astype(BF16)
            for hh in range(2):
                vt_scr[hh, n, 0:HEAD_DIM, :] = vt[hh * HEAD_DIM:(hh + 1) * HEAD_DIM, :]
                vt_scr[hh, n, HEAD_DIM:, :] = ones

    q_t = q_ref[...].T
    sub = lax.broadcasted_iota(jnp.int32, q_t.shape, 0)
    klane = lax.broadcasted_iota(jnp.int32, (nb, LANES), 1)
    blk_id = lax.broadcasted_iota(jnp.int32, (nb, blk), 0)
    past = blk_id < qi

    for hh in range(2):
        km = jnp.where(_pair_head(klane) == hh, kmean_scr[...], 0.0)
        gate = jnp.dot(km, q_t, precision=lax.Precision.HIGHEST, preferred_element_type=F32)
        gm = jnp.where(past, gate, -jnp.inf)
        cnt = jnp.zeros((nb, blk), F32)
        for n in range(nb):
            gn = gm[n:n + 1, :]
            beats = (gn > gm) | ((gn == gm) & (blk_id > n))
            cnt = cnt + jnp.where(beats & (qi > n), 1.0, 0.0)
        keep = (past & (cnt < float(MOBA_TOPK))) | (blk_id == qi)
        pen = jnp.where(keep, 0.0, NEG)
        pen = jnp.concatenate([pen, jnp.zeros((LANES - nb, blk), F32)], axis=0)
        qh = jnp.where(_pair_head(sub) == hh, q_t, 0.0) * qscale
        qaug_scr[hh] = jnp.concatenate([qh.astype(BF16), pen.astype(BF16)], axis=0)
        acc_scr[hh] = jnp.zeros(acc_scr.shape[1:], F32)

    def qk(t, s_ref):
        start = pl.multiple_of(t * blk, blk)
        kb = kaug_scr[pl.ds(start, blk), :]
        for hh in range(2):
            s_ref[hh] = jnp.dot(kb, qaug_scr[hh], preferred_element_type=F32)

    def process(t, s_ref, ms, tail):
        out = []
        for hh in range(2):
            s = s_ref[hh]
            if tail:
                s = jnp.where(key_minus_qry <= (qi - t) * blk, s, NEG)
            m_new = jnp.maximum(ms[hh], jnp.max(s, axis=0, keepdims=True))
            alpha = jnp.exp2(ms[hh] - m_new)
            p = jnp.exp2(s - m_new).astype(BF16)
            pv = jnp.dot(vt_scr[hh, jnp.minimum(t, qi)], p, preferred_element_type=F32)
            acc_scr[hh] = alpha * acc_scr[hh] + pv
            out.append(m_new)
        return tuple(out)

    qk(0, sa_scr)

    def body(k, ms):
        t = 2 * k
        qk(t + 1, sb_scr)
        ms = process(t, sa_scr, ms, False)
        qk(t + 2, sa_scr)
        return process(t + 1, sb_scr, ms, False)

    m_init = jnp.full((1, blk), NEG, F32)
    ms = lax.fori_loop(0, qi // 2, body, (m_init, m_init))
    key_minus_qry = (lax.broadcasted_iota(jnp.int32, (blk, blk), 0)
                     - lax.broadcasted_iota(jnp.int32, (blk, blk), 1))
    t1 = 2 * (qi // 2)
    qk(jnp.minimum(t1 + 1, qi), sb_scr)
    ms = process(t1, sa_scr, ms, True)
    process(t1 + 1, sb_scr, ms, True)

    outs = [acc_scr[hh][:HEAD_DIM] / acc_scr[hh][HEAD_DIM:HEAD_DIM + 1] for hh in range(2)]
    out = jnp.concatenate(outs, axis=0).T
    o_ref[...] = (out * _silu(g_ref[...])).astype(o_ref.dtype)


def _moba(proj3):
    b, seq, _ = proj3.shape
    blk = MOBA_BLOCK
    nb = seq // blk
    npair = MOBA_W // LANES
    per = PROJ_TILE // LANES
    return pl.pallas_call(
        functools.partial(_moba_kernel, nb=nb),
        grid=(b, npair, nb),
        in_specs=[
            pl.BlockSpec((None, blk, LANES), lambda bi, hp, qi: (bi, qi, T_MQ * per + hp)),
            pl.BlockSpec((None, seq, LANES), lambda bi, hp, qi: (bi, 0, T_MK * per + hp)),
            pl.BlockSpec((None, seq, LANES), lambda bi, hp, qi: (bi, 0, T_MV * per + hp)),
            pl.BlockSpec((None, blk, LANES), lambda bi, hp, qi: (bi, qi, T_MG * per + hp)),
        ],
        out_specs=pl.BlockSpec((None, blk, LANES), lambda bi, hp, qi: (bi, qi, hp)),
        out_shape=jax.ShapeDtypeStruct((b, seq, MOBA_W), BF16),
        scratch_shapes=[pltpu.VMEM((seq, 2 * LANES), BF16),
                        pltpu.VMEM((2, nb, HEAD_DIM + MOBA_ONES_ROWS, blk), BF16),
                        pltpu.VMEM((nb, LANES), F32),
                        pltpu.VMEM((2, 2 * LANES, blk), BF16),
                        pltpu.VMEM((2, HEAD_DIM + MOBA_ONES_ROWS, blk), F32),
                        pltpu.VMEM((2, blk, blk), F32),
                        pltpu.VMEM((2, blk, blk), F32)],
        compiler_params=pltpu.CompilerParams(
            dimension_semantics=("parallel", "parallel", "arbitrary"),
            vmem_limit_bytes=VMEM_LIMIT_BYTES),
        name="moba",
    )(proj3, proj3, proj3, proj3)


def _swa_kernel(sink_ref, q_ref, kp_ref, kc_ref, vp_ref, vc_ref, g_ref, o_ref):
    n = pl.program_id(1)
    w = SWA_WINDOW
    scale = HEAD_DIM ** -0.5
    kcat = jnp.concatenate([kp_ref[...], kc_ref[...]], axis=0).astype(BF16)
    vcat = jnp.concatenate([vp_ref[...], vc_ref[...]], axis=0).astype(BF16)
    row = lax.broadcasted_iota(jnp.int32, (w, 2 * w), 0)
    col = lax.broadcasted_iota(jnp.int32, (w, 2 * w), 1)
    band = (col > row) & (col <= row + w)
    lane = lax.broadcasted_iota(jnp.int32, (w, LANES), 1)
    for win in range(SWA_TQ // w):
        valid = band & ((n > 0) | (col >= w)) if win == 0 else band
        kw = kcat[win * w:(win + 2) * w]
        vw = vcat[win * w:(win + 2) * w]
        rows = slice(win * w, (win + 1) * w)
        for i in range(SWA_W // LANES):
            q_t = q_ref[rows, i * LANES:(i + 1) * LANES]
            halves = []
            for c in range(2):
                qm = (jnp.where(_pair_head(lane) == c, q_t, 0.0) * scale).astype(BF16)
                s = _nt_dot(qm, kw)
                s = jnp.where(valid, s, NEG)
                sink = sink_ref[2 * i + c]
                m = jnp.maximum(jnp.max(s, axis=-1, keepdims=True), sink)
                e = jnp.exp(s - m)
                denom = jnp.sum(e, axis=-1, keepdims=True) + jnp.exp(sink - m)
                o = jnp.dot(e.astype(BF16), vw, preferred_element_type=F32)
                halves.append(o / denom)
            out = jnp.where(lane < HEAD_DIM, halves[0], halves[1])
            g_t = g_ref[rows, i * LANES:(i + 1) * LANES]
            o_ref[rows, i * LANES:(i + 1) * LANES] = (out * _silu(g_t)).astype(o_ref.dtype)


def _swa(proj3, sinks_perm):
    b, seq, _ = proj3.shape
    w = SWA_WINDOW
    tq = SWA_TQ
    per = tq // w
    prev = lambda bi, n: (bi, jnp.maximum(n * per - 1, 0), TAIL_K_BLK)
    prev_v = lambda bi, n: (bi, jnp.maximum(n * per - 1, 0), TAIL_V_BLK)
    return pl.pallas_call(
        _swa_kernel,
        grid=(b, seq // tq),
        in_specs=[
            pl.BlockSpec(memory_space=pltpu.SMEM),
            pl.BlockSpec((None, tq, SWA_W), lambda bi, n: (bi, n, T_SQ)),
            pl.BlockSpec((None, w, LANES), prev),
            pl.BlockSpec((None, tq, LANES), lambda bi, n: (bi, n, TAIL_K_BLK)),
            pl.BlockSpec((None, w, LANES), prev_v),
            pl.BlockSpec((None, tq, LANES), lambda bi, n: (bi, n, TAIL_V_BLK)),
            pl.BlockSpec((None, tq, SWA_W), lambda bi, n: (bi, n, T_SG)),
        ],
        out_specs=pl.BlockSpec((None, tq, SWA_W), lambda bi, n: (bi, n, 0)),
        out_shape=jax.ShapeDtypeStruct((b, seq, SWA_W), BF16),
        compiler_params=pltpu.CompilerParams(
            dimension_semantics=("parallel", "arbitrary"),
            vmem_limit_bytes=VMEM_LIMIT_BYTES),
        name="swa",
    )(sinks_perm, proj3, proj3, proj3, proj3, proj3, proj3)


def _ssd_kernel(xbc_ref, z_ref, dt_ref, cw_ref, cb_ref, dtb_ref, alog_ref, dsk_ref, nw_ref,
                o_ref, xbuf, state):
    c = pl.program_id(1)
    s = SSD_CHUNK
    pad = SUBLANES
    hi = lax.Precision.HIGHEST

    @pl.when(c == 0)
    def _():
        xbuf[0:pad, :] = jnp.zeros((pad, SSD_CONV_CH), F32)
        state[...] = jnp.zeros_like(state)

    @pl.when(c > 0)
    def _():
        xbuf[0:pad, :] = xbuf[s:s + pad, :]

    xbuf[pad:pad + s, :] = xbc_ref[...]
    conv = cb_ref[...]
    for k in range(SSD_CONV):
        off = pad - (SSD_CONV - 1) + k
        conv = conv + cw_ref[k:k + 1, :] * xbuf[off:off + s, :]
    xbc = _silu(conv)
    xs = xbc[:, :SSD_W]
    gw = SSD_GROUPS * SSD_STATE
    bm = xbc[:, SSD_W:SSD_W + gw]
    cm = xbc[:, SSD_W + gw:]

    dtr = dt_ref[...] + dtb_ref[...]
    dt = jnp.maximum(dtr, 0.0) + jnp.log(1.0 + jnp.exp(-jnp.abs(dtr)))
    a = dt * (-jnp.exp(alog_ref[...]))
    row = lax.broadcasted_iota(jnp.int32, (s, s), 0)
    col = lax.broadcasted_iota(jnp.int32, (s, s), 1)
    lower = row >= col
    tri = jnp.where(lower, 1.0, 0.0)
    acum = jnp.dot(tri, a, precision=hi, preferred_element_type=F32)
    acum_t = acum.T
    alast = acum[s - 1:s, :]

    erow = lax.broadcasted_iota(jnp.int32, (LANES, SSD_W), 0)
    ecol = lax.broadcasted_iota(jnp.int32, (LANES, SSD_W), 1)
    expand = jnp.where(erow == ecol // SSD_HEAD_DIM, 1.0, 0.0)
    ex = lambda t: jnp.dot(t, expand, precision=hi, preferred_element_type=F32)
    dt_e = ex(dt)
    dec_e = ex(jnp.exp(alast - acum))
    ea_e = ex(jnp.exp(acum))
    cd_e = ex(jnp.broadcast_to(jnp.exp(alast), (SUBLANES, LANES)))[0:1, :]

    x_dt = xs * dt_e
    x_dec = (x_dt * dec_e).astype(BF16)
    x_dt16 = x_dt.astype(BF16)
    lane = lax.broadcasted_iota(jnp.int32, (s, LANES), 1)
    hpg = SSD_HEADS // SSD_GROUPS
    gwid = hpg * SSD_HEAD_DIM
    y_parts = []
    for g in range(SSD_GROUPS):
        bg = bm[:, g * SSD_STATE:(g + 1) * SSD_STATE].astype(BF16)
        cg = cm[:, g * SSD_STATE:(g + 1) * SSD_STATE].astype(BF16)
        cbm = _nt_dot(cg, bg)
        st = state[:, g * gwid:(g + 1) * gwid]
        y_off = jnp.dot(cg, st.astype(BF16), preferred_element_type=F32) \
            * ea_e[:, g * gwid:(g + 1) * gwid]
        new_st = cd_e[:, g * gwid:(g + 1) * gwid] * st + _tn_dot(bg, x_dec[:, g * gwid:(g + 1) * gwid])
        state[:, g * gwid:(g + 1) * gwid] = new_st
        for i in range(hpg // 2):
            lo = g * gwid + i * LANES
            xp = x_dt16[:, lo:lo + LANES]
            ys = []
            for hh in range(2):
                h = g * hpg + 2 * i + hh
                seg = acum[:, h:h + 1] - acum_t[h:h + 1, :]
                lm = jnp.where(lower, jnp.exp(jnp.minimum(seg, 0.0)), 0.0)
                ys.append(jnp.dot((cbm * lm).astype(BF16), xp, preferred_element_type=F32))
            y_parts.append(jnp.where(lane < SSD_HEAD_DIM, ys[0], ys[1])
                           + y_off[:, i * LANES:(i + 1) * LANES])
    y = jnp.concatenate(y_parts, axis=1)
    y = (y + dsk_ref[...] * xs) * _silu(z_ref[...])
    outs = []
    for g in range(SSD_GROUPS):
        yg = y[:, g * gwid:(g + 1) * gwid]
        outs.append(yg * lax.rsqrt(jnp.mean(yg * yg, axis=-1, keepdims=True) + NORM_EPS))
    o_ref[...] = (jnp.concatenate(outs, axis=1) * nw_ref[...]).astype(o_ref.dtype)


def _ssd(proj3, conv_w, conv_b, dt_bias_p, a_log_p, d_skip_e, norm_w):
    b, seq, _ = proj3.shape
    s = SSD_CHUNK
    full = lambda shape: pl.BlockSpec(shape, lambda bi, c: (0, 0))
    return pl.pallas_call(
        _ssd_kernel,
        grid=(b, seq // s),
        in_specs=[
            pl.BlockSpec((None, s, SSD_CONV_CH), lambda bi, c: (bi, c, T_XBC * PROJ_TILE // SSD_CONV_CH)),
            pl.BlockSpec((None, s, SSD_W), lambda bi, c: (bi, c, T_Z)),
            pl.BlockSpec((None, s, LANES), lambda bi, c: (bi, c, TAIL_DT_BLK)),
            full((SSD_CONV, SSD_CONV_CH)),
            full((1, SSD_CONV_CH)),
            full((1, LANES)),
            full((1, LANES)),
            full((1, SSD_W)),
            full((1, SSD_W)),
        ],
        out_specs=pl.BlockSpec((None, s, SSD_W), lambda bi, c: (bi, c, 0)),
        out_shape=jax.ShapeDtypeStruct((b, seq, SSD_W), BF16),
        scratch_shapes=[pltpu.VMEM((s + 2 * SUBLANES, SSD_CONV_CH), F32),
                        pltpu.VMEM((SSD_STATE, SSD_W), F32)],
        compiler_params=pltpu.CompilerParams(
            dimension_semantics=("parallel", "arbitrary"),
            vmem_limit_bytes=VMEM_LIMIT_BYTES),
        name="ssd",
    )(proj3, proj3, proj3, conv_w, conv_b, dt_bias_p, a_log_p, d_skip_e, norm_w)


def _s5_kernel(u_ref, w_ref, k_ref, v_ref, are_ref, aim_ref, y_ref, x_scr, e_scr, s_scr, *, nrow):
    t = S5_CHUNK
    half = S5_TILE_GROUPS * S5_STATE
    for j in range(t):
        x_scr[j] = u_ref[pl.ds(j, nrow, stride=t), :].astype(BF16)
    e = jnp.dot(x_scr[0], w_ref[0], preferred_element_type=F32)
    for j in range(1, t):
        e = e + jnp.dot(x_scr[j], w_ref[j], preferred_element_type=F32)
    e_scr[...] = e
    a_re = are_ref[...]
    a_im = aim_ref[...]

    def body(it, carry):
        s_re, s_im = carry
        r0 = pl.multiple_of(it * S5_SCAN_ROWS, S5_SCAN_ROWS)
        e8 = e_scr[pl.ds(r0, S5_SCAN_ROWS), :]
        rows = []
        for kk in range(S5_SCAN_ROWS):
            rows.append(jnp.concatenate([s_re, s_im], axis=1))
            e_re, e_im = e8[kk:kk + 1, :half], e8[kk:kk + 1, half:]
            s_re, s_im = a_re * s_re - a_im * s_im + e_re, a_re * s_im + a_im * s_re + e_im
        s_scr[pl.ds(r0, S5_SCAN_ROWS), :] = jnp.concatenate(rows, axis=0)
        return s_re, s_im

    zero = jnp.zeros((1, half), F32)
    lax.fori_loop(0, nrow // S5_SCAN_ROWS, body, (zero, zero))
    s16 = s_scr[...].astype(BF16)
    for i in range(t):
        y = jnp.dot(s16, v_ref[i], preferred_element_type=F32)
        for j in range(i + 1):
            y = y + jnp.dot(x_scr[j], k_ref[i - j], preferred_element_type=F32)
        y_ref[pl.ds(i, nrow, stride=t), :] = y


def _s5_core(proj3, w, k, v, a_re, a_im):
    b, seq, _ = proj3.shape
    t = S5_CHUNK
    nrow = seq // t
    ntile = S5_W // LANES
    half = S5_TILE_GROUPS * S5_STATE
    per = PROJ_TILE // LANES
    return pl.pallas_call(
        functools.partial(_s5_kernel, nrow=nrow),
        grid=(ntile, b),
        in_specs=[
            pl.BlockSpec((None, seq, LANES), lambda q, bi: (bi, 0, T_SU * per + q)),
            pl.BlockSpec((None, t, LANES, 2 * half), lambda q, bi: (q, 0, 0, 0)),
            pl.BlockSpec((None, t, LANES, LANES), lambda q, bi: (q, 0, 0, 0)),
            pl.BlockSpec((None, t, 2 * half, LANES), lambda q, bi: (q, 0, 0, 0)),
            pl.BlockSpec((None, 1, half), lambda q, bi: (q, 0, 0)),
            pl.BlockSpec((None, 1, half), lambda q, bi: (q, 0, 0)),
        ],
        out_specs=pl.BlockSpec((None, seq, LANES), lambda q, bi: (bi, 0, q)),
        out_shape=jax.ShapeDtypeStruct((b, seq, S5_W), F32),
        scratch_shapes=[pltpu.VMEM((t, nrow, LANES), BF16),
                        pltpu.VMEM((nrow, 2 * half), F32),
                        pltpu.VMEM((nrow, 2 * half), F32)],
        compiler_params=pltpu.CompilerParams(
            dimension_semantics=("parallel", "parallel"),
            vmem_limit_bytes=VMEM_LIMIT_BYTES),
        name="s5",
    )(proj3, w, k, v, a_re, a_im)


def _s5_tables(a_re, a_im, log_dt, b_re, b_im, c_re, c_im):
    t = S5_CHUNK
    hi = lax.Precision.HIGHEST
    step = jnp.exp(log_dt)[:, None]
    k = jnp.arange(t + 1, dtype=F32)[None, :, None]
    mag = jnp.exp(k * (a_re * step)[:, None, :])
    ang = k * (a_im * step)[:, None, :]
    pw_re, pw_im = mag * jnp.cos(ang), mag * jnp.sin(ang)
    ab_re, ab_im = pw_re[:, 1], pw_im[:, 1]
    den = a_re * a_re + a_im * a_im
    n_re, n_im = ab_re - 1.0, ab_im
    cf_re = (n_re * a_re + n_im * a_im) / den
    cf_im = (n_im * a_re - n_re * a_im) / den
    bb_re = cf_re[..., None] * b_re - cf_im[..., None] * b_im
    bb_im = cf_re[..., None] * b_im + cf_im[..., None] * b_re
    cp_re = c_re[:, None] * pw_re[:, :t, None, :] - c_im[:, None] * pw_im[:, :t, None, :]
    cp_im = c_re[:, None] * pw_im[:, :t, None, :] + c_im[:, None] * pw_re[:, :t, None, :]
    kern = (jnp.einsum('gshp,gpk->gshk', cp_re, bb_re, precision=hi)
            - jnp.einsum('gshp,gpk->gshk', cp_im, bb_im, precision=hi))
    g = a_re.shape[0]
    tg = S5_TILE_GROUPS
    nt = g // tg
    eye = jnp.eye(tg, dtype=F32)

    def tile_diag(m, pattern):
        m5 = m.reshape(nt, tg, t, m.shape[2], m.shape[3])
        out = jnp.einsum(pattern, m5, eye)
        return out.reshape(nt, t, out.shape[2] * out.shape[3], out.shape[4] * out.shape[5])

    k_bd = tile_diag(kern, 'qgsab,gk->qsgbka')
    k_rev = (t - 1) - jnp.arange(t, dtype=F32)[None, :, None]
    mag_r = jnp.exp(k_rev * (a_re * step)[:, None, :])
    ang_r = k_rev * (a_im * step)[:, None, :]
    rp_re, rp_im = mag_r * jnp.cos(ang_r), mag_r * jnp.sin(ang_r)
    bt_re, bt_im = bb_re.transpose(0, 2, 1)[:, None], bb_im.transpose(0, 2, 1)[:, None]
    w_re = rp_re[:, :, None, :] * bt_re - rp_im[:, :, None, :] * bt_im
    w_im = rp_re[:, :, None, :] * bt_im + rp_im[:, :, None, :] * bt_re
    w_bd = jnp.concatenate([tile_diag(w_re, 'qgjhp,gk->qjghkp'),
                            tile_diag(w_im, 'qgjhp,gk->qjghkp')], axis=-1)
    q_re, q_im = pw_re[:, 1:], pw_im[:, 1:]
    v_re = c_re[:, None] * q_re[:, :, None, :] - c_im[:, None] * q_im[:, :, None, :]
    v_im = c_re[:, None] * q_im[:, :, None, :] + c_im[:, None] * q_re[:, :, None, :]
    v_bd = jnp.concatenate([tile_diag(v_re, 'qgihp,gk->qigpkh'),
                            tile_diag(-v_im, 'qgihp,gk->qigpkh')], axis=2)
    at_re = pw_re[:, t].reshape(nt, 1, tg * S5_STATE)
    at_im = pw_im[:, t].reshape(nt, 1, tg * S5_STATE)
    return w_bd.astype(BF16), k_bd.astype(BF16), v_bd.astype(BF16), at_re, at_im


def _outproj_kernel(ym_ref, ys_ref, yw_ref, y5_ref, u_ref, g5_ref, x_ref, w_ref, pg_ref,
                    d5_ref, gw_ref, gb_ref, o_ref):
    y5 = y5_ref[...] + d5_ref[...] * u_ref[...]
    c0 = math.sqrt(2.0 / math.pi)
    y5 = y5 * (0.5 * (1.0 + jnp.tanh(c0 * (y5 + 0.044715 * (y5 * y5 * y5)))))
    gl = jnp.dot(y5.astype(BF16), gw_ref[...], preferred_element_type=F32) + gb_ref[...]
    y5 = y5 * _sigmoid(gl) * _silu(g5_ref[...])
    parts = [ym_ref[...], ys_ref[...], yw_ref[...], y5.astype(BF16)]
    acc = None
    for i, part in enumerate(parts):
        d = jnp.dot(part, w_ref[i * PROJ_TILE:(i + 1) * PROJ_TILE, :], preferred_element_type=F32)
        acc = d if acc is None else acc + d
    ms = jnp.mean(acc * acc, axis=-1, keepdims=True)
    o_ref[...] = x_ref[...] + acc * lax.rsqrt(ms + NORM_EPS) * pg_ref[...]


def _outproj(y_moba, y_ssd, y_swa, y_s5, proj, x2, w_out, post_g, s5_d, glu_w, glu_b, tm):
    m = x2.shape[0]
    rows = lambda w: pl.BlockSpec((tm, w), lambda i: (i, 0))
    full = lambda shape: pl.BlockSpec(shape, lambda i: (0, 0))
    return pl.pallas_call(
        _outproj_kernel,
        grid=(m // tm,),
        in_specs=[
            rows(MOBA_W), rows(SSD_W), rows(SWA_W), rows(S5_W),
            pl.BlockSpec((tm, PROJ_TILE), lambda i: (i, T_SU)),
            pl.BlockSpec((tm, PROJ_TILE), lambda i: (i, T_S5G)),
            rows(D_MODEL),
            full((MIX_W, D_MODEL)),
            full((1, D_MODEL)),
            full((1, S5_W)),
            full((S5_W, S5_W)),
            full((1, S5_W)),
        ],
        out_specs=rows(D_MODEL),
        out_shape=jax.ShapeDtypeStruct((m, D_MODEL), F32),
        compiler_params=pltpu.CompilerParams(
            dimension_semantics=("parallel",),
            vmem_limit_bytes=VMEM_LIMIT_BYTES),
        name="outproj",
    )(y_moba, y_ssd, y_swa, y_s5, proj, proj, x2, w_out, post_g, s5_d, glu_w, glu_b)


def _swa_tiles(w, axis):
    s = w.shape
    per_kv = SWA_HEADS // SWA_KV_HEADS
    w = w.reshape(s[:axis] + (SWA_KV_HEADS, per_kv, HEAD_DIM) + s[axis + 1:])
    return jnp.swapaxes(w, axis, axis + 1).reshape(s)


def _rotary_pairs(w):
    s = w.shape
    w = w.reshape(s[:-1] + (s[-1] // LANES, 2, 2, HEAD_DIM // 2))
    return jnp.swapaxes(w, -2, -3).reshape(s)


def _rope_tables(seq):
    inv = 1.0 / (ROPE_THETA ** (jnp.arange(0, HEAD_DIM, 2, dtype=F32) / HEAD_DIM))
    ang = jnp.arange(seq, dtype=F32)[:, None] * inv[None, :]
    cos, sin = jnp.cos(ang), jnp.sin(ang)
    cos_t = jnp.concatenate([cos, cos, cos, cos], axis=1)
    sin_t = jnp.concatenate([-sin, -sin, sin, sin], axis=1)
    reps = PROJ_TILE // LANES
    return jnp.tile(cos_t, (1, reps)), jnp.tile(sin_t, (1, reps))


def _cat_w_in(w_in):
    offs = np.cumsum([0, MOBA_W, MOBA_W, MOBA_W, MOBA_W, SSD_CONV_CH, SSD_HEADS, SSD_W,
                      SWA_W, SWA_KV_W, SWA_KV_W, SWA_W, S5_W, S5_W])
    (o_mq, o_mk, o_mv, o_mg, o_xbc, o_dt, o_z, o_sq, o_sk, o_sv, o_sg, o_su, o_s5g, _) = [int(v) for v in offs]
    sl = lambda o, w: w_in[..., o:o + w]
    pad = jnp.zeros(w_in.shape[:-1] + (PROJ_TILE - 2 * SWA_KV_W - SSD_HEADS,), w_in.dtype)
    pieces = [_rotary_pairs(sl(o_mq, MOBA_W)), _rotary_pairs(sl(o_mk, MOBA_W)),
              sl(o_mv, MOBA_W), sl(o_mg, MOBA_W),
              sl(o_xbc, SSD_CONV_CH), sl(o_z, SSD_W),
              _rotary_pairs(_swa_tiles(sl(o_sq, SWA_W), 2)), _swa_tiles(sl(o_sg, SWA_W), 2),
              sl(o_su, S5_W), sl(o_s5g, S5_W),
              _rotary_pairs(sl(o_sk, SWA_KV_W)), sl(o_sv, SWA_KV_W), sl(o_dt, SSD_HEADS), pad]
    return jnp.concatenate(pieces, axis=-1).astype(BF16)


def _pad_lanes(v):
    return jnp.pad(v, (0, LANES - v.shape[0]))[None, :]


def _layer(x2, b, seq, cos_t, sin_t, pre_g, post_g, w_in, w_out, conv_w, conv_b, dt_bias, a_log,
           ssd_d, ssd_norm, sinks, s5_tabs, s5_d, glu_w, glu_b):
    proj = _inproj(x2, pre_g[None, :], w_in, cos_t, sin_t, seq, min(TM_INPROJ, seq))
    proj3 = proj.reshape(b, seq, PROJ_W)

    y_moba = _moba(proj3)
    y_swa = _swa(proj3, sinks)
    y_ssd = _ssd(proj3, conv_w, conv_b[None, :], _pad_lanes(dt_bias), _pad_lanes(a_log),
                 jnp.repeat(ssd_d, SSD_HEAD_DIM)[None, :], ssd_norm[None, :])
    y_s5 = _s5_core(proj3, *s5_tabs).reshape(b * seq, S5_W)

    m = b * seq
    return _outproj(y_moba.reshape(m, MOBA_W), y_ssd.reshape(m, SSD_W), y_swa.reshape(m, SWA_W),
                    y_s5, proj, x2, w_out, post_g[None, :], s5_d[None, :],
                    glu_w, glu_b[None, :], min(TM_OUTPROJ, seq))


def kernel(x, pre_norm, post_norm, w_in, w_out, ssd_conv_w, ssd_conv_b, ssd_dt_bias, ssd_a_log, ssd_d, ssd_norm, swa_sinks, s5_a_re, s5_a_im, s5_log_dt, s5_b_re, s5_b_im, s5_c_re, s5_c_im, s5_d, s5_glu_w, s5_glu_b):
    b, seq, d = x.shape
    assert d == D_MODEL
    assert seq % MOBA_BLOCK == 0 and seq % SSD_CHUNK == 0 and seq % SWA_WINDOW == 0
    assert seq % (S5_CHUNK * S5_SCAN_ROWS) == 0
    assert seq % SWA_TQ == 0
    depth = pre_norm.shape[0]
    cos_t, sin_t = _rope_tables(seq)
    w_cat = _cat_w_in(w_in)
    swa_lo = MOBA_W + SSD_W
    w_out_p = jnp.concatenate([w_out[:, :swa_lo], _swa_tiles(w_out[:, swa_lo:swa_lo + SWA_W], 1),
                               w_out[:, swa_lo + SWA_W:]], axis=1).astype(BF16)
    per_kv = SWA_HEADS // SWA_KV_HEADS
    sinks_p = jnp.swapaxes(swa_sinks.reshape(depth, SWA_KV_HEADS, per_kv), 1, 2).reshape(depth, SWA_HEADS)
    glu_w16 = s5_glu_w.astype(BF16)
    fold = lambda a: a.reshape((depth * S5_GROUPS,) + a.shape[2:])
    tabs = _s5_tables(fold(s5_a_re), fold(s5_a_im), s5_log_dt.reshape(-1), fold(s5_b_re), fold(s5_b_im),
                      fold(s5_c_re), fold(s5_c_im))
    tabs = [a.reshape((depth, a.shape[0] // depth) + a.shape[1:]) for a in tabs]

    x2 = x.reshape(b * seq, d)
    for l in range(depth):
        x2 = _layer(x2, b, seq, cos_t, sin_t, pre_norm[l], post_norm[l], w_cat[l], w_out_p[l],
                    ssd_conv_w[l], ssd_conv_b[l], ssd_dt_bias[l], ssd_a_log[l], ssd_d[l], ssd_norm[l],
                    sinks_p[l], [a[l] for a in tabs], s5_d[l], glu_w16[l], s5_glu_b[l])
    return x2.reshape(b, seq, d)
```

```python
import functools
import math

import jax
import jax.numpy as jnp
import numpy as np
from jax import lax
from jax.experimental import pallas as pl
from jax.experimental.pallas import tpu as pltpu

F32 = jnp.float32
BF16 = jnp.bfloat16

D_MODEL = 2048
HEAD_DIM = 64
ROPE_THETA = 10000.0
NORM_EPS = 1e-6
MOBA_HEADS = 8
MOBA_W = MOBA_HEADS * HEAD_DIM
MOBA_BLOCK = 256
MOBA_TOPK = 3
SSD_HEADS = 8
SSD_HEAD_DIM = 64
SSD_W = SSD_HEADS * SSD_HEAD_DIM
SSD_GROUPS = 2
SSD_STATE = 128
SSD_CONV = 4
SSD_CHUNK = 256
SSD_CONV_CH = SSD_W + 2 * SSD_GROUPS * SSD_STATE
SWA_HEADS = 8
SWA_KV_HEADS = 2
SWA_W = SWA_HEADS * HEAD_DIM
SWA_KV_W = SWA_KV_HEADS * HEAD_DIM
SWA_WINDOW = 128
S5_W = 512
S5_GROUP = 16
S5_GROUPS = S5_W // S5_GROUP
S5_STATE = 64
MIX_W = MOBA_W + SSD_W + SWA_W + S5_W

LANES = 128
SUBLANES = 8
VMEM_LIMIT_BYTES = 48 * 1024 * 1024

PROJ_TILE = 512
T_MQ, T_MK, T_MV, T_MG, T_XBC, T_Z, T_SQ, T_SG, T_SU, T_S5G, T_TAIL = 0, 1, 2, 3, 4, 6, 7, 8, 9, 10, 11
PROJ_W = 12 * PROJ_TILE
TAIL_K_BLK = T_TAIL * (PROJ_TILE // LANES)
TAIL_V_BLK = TAIL_K_BLK + 1
TAIL_DT_BLK = TAIL_K_BLK + 2

NEG = -1e30
TM_INPROJ = 1024
TM_OUTPROJ = 512
SWA_TQ = 512
MOBA_PAIRS = 4
MOBA_ONES_ROWS = 16
S5_CHUNK = 8
S5_TILE_GROUPS = LANES // S5_GROUP
S5_SCAN_ROWS = SUBLANES


def _silu(x):
    return x / (1.0 + jnp.exp(-x))


def _sigmoid(x):
    return 1.0 / (1.0 + jnp.exp(-x))


def _bf16_parts(x):
    hi = x.astype(BF16)
    r1 = x - hi.astype(F32)
    mid = r1.astype(BF16)
    lo = (r1 - mid.astype(F32)).astype(BF16)
    return hi, mid, lo


def _nt_dot(a, b, precision=None):
    return lax.dot_general(a, b, (((1,), (1,)), ((), ())), precision=precision,
                           preferred_element_type=F32)


def _tn_dot(a, b, precision=None):
    return lax.dot_general(a, b, (((0,), (0,)), ((), ())), precision=precision,
                           preferred_element_type=F32)


def _inproj_kernel(x_ref, g_ref, w_ref, cos_ref, sin_ref, o_ref, h_scr):
    j = pl.program_id(1)

    @pl.when(j == 0)
    def _():
        x = x_ref[...]
        ms = jnp.mean(x * x, axis=-1, keepdims=True)
        h_scr[...] = (x * lax.rsqrt(ms + NORM_EPS) * g_ref[...]).astype(BF16)

    acc = jnp.dot(h_scr[...], w_ref[...], preferred_element_type=F32)
    is_rope = (j == T_MQ) | (j == T_MK) | (j == T_SQ) | (j == T_TAIL)

    @pl.when(is_rope)
    def _():
        rot = jnp.concatenate(
            [pltpu.roll(acc[:, t * LANES:(t + 1) * LANES], LANES // 2, axis=1)
             for t in range(acc.shape[1] // LANES)], axis=1)
        roped = acc * cos_ref[...] + rot * sin_ref[...]
        lane = lax.broadcasted_iota(jnp.int32, acc.shape, 1)
        apply = (j != T_TAIL) | (lane < SWA_KV_W)
        o_ref[...] = jnp.where(apply, roped, acc)

    @pl.when(jnp.logical_not(is_rope))
    def _():
        o_ref[...] = acc


def _inproj(x2, pre_g, w_cat, cos_t, sin_t, seq, tm):
    m = x2.shape[0]
    nseq = seq // tm
    return pl.pallas_call(
        _inproj_kernel,
        grid=(m // tm, PROJ_W // PROJ_TILE),
        in_specs=[
            pl.BlockSpec((tm, D_MODEL), lambda i, j: (i, 0)),
            pl.BlockSpec((1, D_MODEL), lambda i, j: (0, 0)),
            pl.BlockSpec((D_MODEL, PROJ_TILE), lambda i, j: (0, j)),
            pl.BlockSpec((tm, PROJ_TILE), lambda i, j: (i % nseq, 0)),
            pl.BlockSpec((tm, PROJ_TILE), lambda i, j: (i % nseq, 0)),
        ],
        out_specs=pl.BlockSpec((tm, PROJ_TILE), lambda i, j: (i, j)),
        out_shape=jax.ShapeDtypeStruct((m, PROJ_W), F32),
        scratch_shapes=[pltpu.VMEM((tm, D_MODEL), BF16)],
        compiler_params=pltpu.CompilerParams(
            dimension_semantics=("parallel", "arbitrary"),
            vmem_limit_bytes=VMEM_LIMIT_BYTES),
        name="inproj",
    )(x2, pre_g, w_cat, cos_t, sin_t)


def _pair_head(idx):
    return (idx // (HEAD_DIM // 2)) % 2


def _moba_kernel(q_ref, k_ref, v_ref, g_ref, o_ref, kaug_scr, vt_scr, kmean_scr, qaug_scr,
                 acc_scr, sa_scr, sb_scr, *, nb):
    qi = pl.program_id(2)
    blk = MOBA_BLOCK
    qscale = (HEAD_DIM ** -0.5) * math.log2(math.e)

    nh = 2 * MOBA_PAIRS

    @pl.when(qi == 0)
    def _():
        seq = k_ref.shape[0]
        row_blk = lax.broadcasted_iota(jnp.int32, (seq, LANES), 0) // blk
        lane = lax.broadcasted_iota(jnp.int32, (seq, LANES), 1)
        onehot = jnp.where(row_blk == lane, 1.0, 0.0).astype(BF16)
        kmean_scr[...] = jnp.mean(k_ref[...].reshape(nb, blk, MOBA_PAIRS * LANES), axis=1)
        for pp in range(MOBA_PAIRS):
            kaug_scr[pp, :, :LANES] = k_ref[:, pp * LANES:(pp + 1) * LANES].astype(BF16)
            kaug_scr[pp, :, LANES:] = onehot
        ones = jnp.ones((MOBA_ONES_ROWS, blk), BF16)
        for n in range(nb):
            vt = v_ref[n * blk:(n + 1) * blk, :].T.astype(BF16)
            for hd in range(nh):
                vt_scr[hd, n, 0:HEAD_DIM, :] = vt[hd * HEAD_DIM:(hd + 1) * HEAD_DIM, :]
                vt_scr[hd, n, HEAD_DIM:, :] = ones

    q_all = q_ref[...].T
    sub = lax.broadcasted_iota(jnp.int32, (LANES, blk), 0)
    klane = lax.broadcasted_iota(jnp.int32, (nb, LANES), 1)
    blk_id = lax.broadcasted_iota(jnp.int32, (nb, blk), 0)
    past = blk_id < qi

    for hd in range(nh):
        pp, hh = divmod(hd, 2)
        q_t = q_all[pp * LANES:(pp + 1) * LANES]
        km = jnp.where(_pair_head(klane) == hh, kmean_scr[:, pp * LANES:(pp + 1) * LANES], 0.0)
        gate = jnp.dot(km, q_t, precision=lax.Precision.HIGHEST, preferred_element_type=F32)
        gm = jnp.where(past, gate, -jnp.inf)
        cnt = jnp.zeros((nb, blk), F32)
        for n in range(nb):
            gn = gm[n:n + 1, :]
            beats = (gn > gm) | ((gn == gm) & (blk_id > n))
            cnt = cnt + jnp.where(beats & (qi > n), 1.0, 0.0)
        keep = (past & (cnt < float(MOBA_TOPK))) | (blk_id == qi)
        pen = jnp.where(keep, 0.0, NEG)
        pen = jnp.concatenate([pen, jnp.zeros((LANES - nb, blk), F32)], axis=0)
        qh = jnp.where(_pair_head(sub) == hh, q_t, 0.0) * qscale
        qaug_scr[hd] = jnp.concatenate([qh.astype(BF16), pen.astype(BF16)], axis=0)
        acc_scr[hd] = jnp.zeros(acc_scr.shape[1:], F32)

    def qk(t, s_ref):
        start = pl.multiple_of(t * blk, blk)
        for pp in range(MOBA_PAIRS):
            kb = kaug_scr[pp, pl.ds(start, blk), :]
            for hd in (2 * pp, 2 * pp + 1):
                s_ref[hd] = jnp.dot(kb, qaug_scr[hd], preferred_element_type=F32)

    def process(t, s_ref, ms, tail):
        out = []
        for hd in range(nh):
            s = s_ref[hd]
            if tail:
                s = jnp.where(key_minus_qry <= (qi - t) * blk, s, NEG)
            m_new = jnp.maximum(ms[hd], jnp.max(s, axis=0, keepdims=True))
            alpha = jnp.exp2(ms[hd] - m_new)
            p = jnp.exp2(s - m_new).astype(BF16)
            pv = jnp.dot(vt_scr[hd, jnp.minimum(t, qi)], p, preferred_element_type=F32)
            acc_scr[hd] = alpha * acc_scr[hd] + pv
            out.append(m_new)
        return tuple(out)

    qk(0, sa_scr)

    def body(k, ms):
        t = 2 * k
        qk(t + 1, sb_scr)
        ms = process(t, sa_scr, ms, False)
        qk(t + 2, sa_scr)
        return process(t + 1, sb_scr, ms, False)

    m_init = jnp.full((1, blk), NEG, F32)
    ms = lax.fori_loop(0, qi // 2, body, (m_init,) * nh)
    key_minus_qry = (lax.broadcasted_iota(jnp.int32, (blk, blk), 0)
                     - lax.broadcasted_iota(jnp.int32, (blk, blk), 1))
    t1 = 2 * (qi // 2)
    has_second = t1 < qi

    @pl.when(has_second)
    def _():
        qk(t1 + 1, sb_scr)
        process(t1 + 1, sb_scr, process(t1, sa_scr, ms, False), True)

    @pl.when(jnp.logical_not(has_second))
    def _():
        process(t1, sa_scr, ms, True)

    outs = [acc_scr[hd][:HEAD_DIM] / acc_scr[hd][HEAD_DIM:HEAD_DIM + 1] for hd in range(nh)]
    out = jnp.concatenate(outs, axis=0).T
    o_ref[...] = (out * _silu(g_ref[...])).astype(o_ref.dtype)


def _moba(proj3):
    b, seq, _ = proj3.shape
    blk = MOBA_BLOCK
    nb = seq // blk
    wid = MOBA_PAIRS * LANES
    ngrp = MOBA_W // wid
    per = PROJ_TILE // wid
    nh = 2 * MOBA_PAIRS
    return pl.pallas_call(
        functools.partial(_moba_kernel, nb=nb),
        grid=(b, ngrp, nb),
        in_specs=[
            pl.BlockSpec((None, blk, wid), lambda bi, hp, qi: (bi, qi, T_MQ * per + hp)),
            pl.BlockSpec((None, seq, wid), lambda bi, hp, qi: (bi, 0, T_MK * per + hp),
                         pipeline_mode=pl.Buffered(1)),
            pl.BlockSpec((None, seq, wid), lambda bi, hp, qi: (bi, 0, T_MV * per + hp),
                         pipeline_mode=pl.Buffered(1)),
            pl.BlockSpec((None, blk, wid), lambda bi, hp, qi: (bi, qi, T_MG * per + hp)),
        ],
        out_specs=pl.BlockSpec((None, blk, wid), lambda bi, hp, qi: (bi, qi, hp)),
        out_shape=jax.ShapeDtypeStruct((b, seq, MOBA_W), BF16),
        scratch_shapes=[pltpu.VMEM((MOBA_PAIRS, seq, 2 * LANES), BF16),
                        pltpu.VMEM((nh, nb, HEAD_DIM + MOBA_ONES_ROWS, blk), BF16),
                        pltpu.VMEM((nb, wid), F32),
                        pltpu.VMEM((nh, 2 * LANES, blk), BF16),
                        pltpu.VMEM((nh, HEAD_DIM + MOBA_ONES_ROWS, blk), F32),
                        pltpu.VMEM((nh, blk, blk), F32),
                        pltpu.VMEM((nh, blk, blk), F32)],
        compiler_params=pltpu.CompilerParams(
            dimension_semantics=("parallel", "parallel", "arbitrary"),
            vmem_limit_bytes=VMEM_LIMIT_BYTES),
        name="moba",
    )(proj3, proj3, proj3, proj3)


def _swa_kernel(sink_ref, q_ref, kp_ref, kc_ref, vp_ref, vc_ref, g_ref, o_ref):
    n = pl.program_id(1)
    w = SWA_WINDOW
    scale = HEAD_DIM ** -0.5
    kcat = jnp.concatenate([kp_ref[...], kc_ref[...]], axis=0).astype(BF16)
    vcat = jnp.concatenate([vp_ref[...], vc_ref[...]], axis=0).astype(BF16)
    row = lax.broadcasted_iota(jnp.int32, (w, 2 * w), 0)
    col = lax.broadcasted_iota(jnp.int32, (w, 2 * w), 1)
    band = (col > row) & (col <= row + w)
    lane = lax.broadcasted_iota(jnp.int32, (w, LANES), 1)
    for win in range(SWA_TQ // w):
        valid = band & ((n > 0) | (col >= w)) if win == 0 else band
        kw = kcat[win * w:(win + 2) * w]
        vw = vcat[win * w:(win + 2) * w]
        rows = slice(win * w, (win + 1) * w)
        for i in range(SWA_W // LANES):
            q_t = q_ref[rows, i * LANES:(i + 1) * LANES]
            halves = []
            for c in range(2):
                qm = (jnp.where(_pair_head(lane) == c, q_t, 0.0) * scale).astype(BF16)
                s = _nt_dot(qm, kw)
                s = jnp.where(valid, s, NEG)
                sink = sink_ref[2 * i + c]
                m = jnp.maximum(jnp.max(s, axis=-1, keepdims=True), sink)
                e = jnp.exp(s - m)
                denom = jnp.sum(e, axis=-1, keepdims=True) + jnp.exp(sink - m)
                o = jnp.dot(e.astype(BF16), vw, preferred_element_type=F32)
                halves.append(o / denom)
            out = jnp.where(lane < HEAD_DIM, halves[0], halves[1])
            g_t = g_ref[rows, i * LANES:(i + 1) * LANES]
            o_ref[rows, i * LANES:(i + 1) * LANES] = (out * _silu(g_t)).astype(o_ref.dtype)


def _swa(proj3, sinks_perm):
    b, seq, _ = proj3.shape
    w = SWA_WINDOW
    tq = SWA_TQ
    per = tq // w
    prev = lambda bi, n: (bi, jnp.maximum(n * per - 1, 0), TAIL_K_BLK)
    prev_v = lambda bi, n: (bi, jnp.maximum(n * per - 1, 0), TAIL_V_BLK)
    return pl.pallas_call(
        _swa_kernel,
        grid=(b, seq // tq),
        in_specs=[
            pl.BlockSpec(memory_space=pltpu.SMEM),
            pl.BlockSpec((None, tq, SWA_W), lambda bi, n: (bi, n, T_SQ)),
            pl.BlockSpec((None, w, LANES), prev),
            pl.BlockSpec((None, tq, LANES), lambda bi, n: (bi, n, TAIL_K_BLK)),
            pl.BlockSpec((None, w, LANES), prev_v),
            pl.BlockSpec((None, tq, LANES), lambda bi, n: (bi, n, TAIL_V_BLK)),
            pl.BlockSpec((None, tq, SWA_W), lambda bi, n: (bi, n, T_SG)),
        ],
        out_specs=pl.BlockSpec((None, tq, SWA_W), lambda bi, n: (bi, n, 0)),
        out_shape=jax.ShapeDtypeStruct((b, seq, SWA_W), BF16),
        compiler_params=pltpu.CompilerParams(
            dimension_semantics=("parallel", "arbitrary"),
            vmem_limit_bytes=VMEM_LIMIT_BYTES),
        name="swa",
    )(sinks_perm, proj3, proj3, proj3, proj3, proj3, proj3)


def _ssd_kernel(xbc_ref, z_ref, dt_ref, cw_ref, cb_ref, dtb_ref, alog_ref, dsk_ref, nw_ref,
                o_ref, xbuf, state):
    c = pl.program_id(1)
    s = SSD_CHUNK
    pad = SUBLANES
    hi = lax.Precision.HIGHEST

    @pl.when(c == 0)
    def _():
        xbuf[0:pad, :] = jnp.zeros((pad, SSD_CONV_CH), F32)
        state[...] = jnp.zeros_like(state)

    @pl.when(c > 0)
    def _():
        xbuf[0:pad, :] = xbuf[s:s + pad, :]

    xbuf[pad:pad + s, :] = xbc_ref[...]
    conv = cb_ref[...]
    for k in range(SSD_CONV):
        off = pad - (SSD_CONV - 1) + k
        conv = conv + cw_ref[k:k + 1, :] * xbuf[off:off + s, :]
    xbc = _silu(conv)
    xs = xbc[:, :SSD_W]
    gw = SSD_GROUPS * SSD_STATE
    bm = xbc[:, SSD_W:SSD_W + gw]
    cm = xbc[:, SSD_W + gw:]

    dtr = dt_ref[...] + dtb_ref[...]
    dt = jnp.maximum(dtr, 0.0) + jnp.log(1.0 + jnp.exp(-jnp.abs(dtr)))
    a = dt * (-jnp.exp(alog_ref[...]))
    row = lax.broadcasted_iota(jnp.int32, (s, s), 0)
    col = lax.broadcasted_iota(jnp.int32, (s, s), 1)
    lower = row >= col
    tri = jnp.where(lower, 1.0, 0.0).astype(BF16)
    acum = sum(jnp.dot(tri, part, preferred_element_type=F32) for part in _bf16_parts(a))
    acum_t = acum.T
    alast = acum[s - 1:s, :]

    erow = lax.broadcasted_iota(jnp.int32, (LANES, SSD_W), 0)
    ecol = lax.broadcasted_iota(jnp.int32, (LANES, SSD_W), 1)
    expand = jnp.where(erow == ecol // SSD_HEAD_DIM, 1.0, 0.0).astype(BF16)
    cols = jnp.concatenate([dt, jnp.exp(alast - acum), jnp.exp(acum),
                            jnp.broadcast_to(jnp.exp(alast), (SUBLANES, LANES))], axis=0)
    cols_e = sum(jnp.dot(part, expand, preferred_element_type=F32) for part in _bf16_parts(cols))
    dt_e, dec_e, ea_e = cols_e[:s], cols_e[s:2 * s], cols_e[2 * s:3 * s]
    cd_e = cols_e[3 * s:3 * s + 1]

    x_dt = xs * dt_e
    x_dec = (x_dt * dec_e).astype(BF16)
    x_dt16 = x_dt.astype(BF16)
    lane = lax.broadcasted_iota(jnp.int32, (s, LANES), 1)
    hpg = SSD_HEADS // SSD_GROUPS
    gwid = hpg * SSD_HEAD_DIM
    y_parts = []
    for g in range(SSD_GROUPS):
        bg = bm[:, g * SSD_STATE:(g + 1) * SSD_STATE].astype(BF16)
        cg = cm[:, g * SSD_STATE:(g + 1) * SSD_STATE].astype(BF16)
        cbm = _nt_dot(cg, bg)
        st = state[:, g * gwid:(g + 1) * gwid]
        y_off = jnp.dot(cg, st.astype(BF16), preferred_element_type=F32) \
            * ea_e[:, g * gwid:(g + 1) * gwid]
        new_st = cd_e[:, g * gwid:(g + 1) * gwid] * st + _tn_dot(bg, x_dec[:, g * gwid:(g + 1) * gwid])
        state[:, g * gwid:(g + 1) * gwid] = new_st
        for i in range(hpg // 2):
            lo = g * gwid + i * LANES
            xp = x_dt16[:, lo:lo + LANES]
            ys = []
            for hh in range(2):
                h = g * hpg + 2 * i + hh
                seg = acum[:, h:h + 1] - acum_t[h:h + 1, :]
                lm = jnp.where(lower, jnp.exp(jnp.minimum(seg, 0.0)), 0.0)
                ys.append(jnp.dot((cbm * lm).astype(BF16), xp, preferred_element_type=F32))
            y_parts.append(jnp.where(lane < SSD_HEAD_DIM, ys[0], ys[1])
                           + y_off[:, i * LANES:(i + 1) * LANES])
    y = jnp.concatenate(y_parts, axis=1)
    y = (y + dsk_ref[...] * xs) * _silu(z_ref[...])
    outs = []
    for g in range(SSD_GROUPS):
        yg = y[:, g * gwid:(g + 1) * gwid]
        outs.append(yg * lax.rsqrt(jnp.mean(yg * yg, axis=-1, keepdims=True) + NORM_EPS))
    o_ref[...] = (jnp.concatenate(outs, axis=1) * nw_ref[...]).astype(o_ref.dtype)


def _ssd(proj3, conv_w, conv_b, dt_bias_p, a_log_p, d_skip_e, norm_w):
    b, seq, _ = proj3.shape
    s = SSD_CHUNK
    full = lambda shape: pl.BlockSpec(shape, lambda bi, c: (0, 0))
    return pl.pallas_call(
        _ssd_kernel,
        grid=(b, seq // s),
        in_specs=[
            pl.BlockSpec((None, s, SSD_CONV_CH), lambda bi, c: (bi, c, T_XBC * PROJ_TILE // SSD_CONV_CH)),
            pl.BlockSpec((None, s, SSD_W), lambda bi, c: (bi, c, T_Z)),
            pl.BlockSpec((None, s, LANES), lambda bi, c: (bi, c, TAIL_DT_BLK)),
            full((SSD_CONV, SSD_CONV_CH)),
            full((1, SSD_CONV_CH)),
            full((1, LANES)),
            full((1, LANES)),
            full((1, SSD_W)),
            full((1, SSD_W)),
        ],
        out_specs=pl.BlockSpec((None, s, SSD_W), lambda bi, c: (bi, c, 0)),
        out_shape=jax.ShapeDtypeStruct((b, seq, SSD_W), BF16),
        scratch_shapes=[pltpu.VMEM((s + 2 * SUBLANES, SSD_CONV_CH), F32),
                        pltpu.VMEM((SSD_STATE, SSD_W), F32)],
        compiler_params=pltpu.CompilerParams(
            dimension_semantics=("parallel", "arbitrary"),
            vmem_limit_bytes=VMEM_LIMIT_BYTES),
        name="ssd",
    )(proj3, proj3, proj3, conv_w, conv_b, dt_bias_p, a_log_p, d_skip_e, norm_w)


def _s5_kernel(u_ref, w_ref, k_ref, v_ref, are_ref, aim_ref, y_ref, x_scr, e_scr, s_scr, *, nrow):
    t = S5_CHUNK
    half = S5_TILE_GROUPS * S5_STATE
    for j in range(t):
        x_scr[j] = u_ref[pl.ds(j, nrow, stride=t), :].astype(BF16)
    e = jnp.dot(x_scr[0], w_ref[0], preferred_element_type=F32)
    for j in range(1, t):
        e = e + jnp.dot(x_scr[j], w_ref[j], preferred_element_type=F32)
    e_scr[...] = e
    a_re = are_ref[...]
    a_im = aim_ref[...]

    def body(it, carry):
        s_re, s_im = carry
        r0 = pl.multiple_of(it * S5_SCAN_ROWS, S5_SCAN_ROWS)
        e8 = e_scr[pl.ds(r0, S5_SCAN_ROWS), :]
        rows = []
        for kk in range(S5_SCAN_ROWS):
            rows.append(jnp.concatenate([s_re, s_im], axis=1))
            e_re, e_im = e8[kk:kk + 1, :half], e8[kk:kk + 1, half:]
            s_re, s_im = a_re * s_re - a_im * s_im + e_re, a_re * s_im + a_im * s_re + e_im
        s_scr[pl.ds(r0, S5_SCAN_ROWS), :] = jnp.concatenate(rows, axis=0)
        return s_re, s_im

    zero = jnp.zeros((1, half), F32)
    lax.fori_loop(0, nrow // S5_SCAN_ROWS, body, (zero, zero))
    s16 = s_scr[...].astype(BF16)
    for i in range(t):
        y = jnp.dot(s16, v_ref[i], preferred_element_type=F32)
        for j in range(i + 1):
            y = y + jnp.dot(x_scr[j], k_ref[i - j], preferred_element_type=F32)
        y_ref[pl.ds(i, nrow, stride=t), :] = y


def _s5_core(proj3, w, k, v, a_re, a_im):
    b, seq, _ = proj3.shape
    t = S5_CHUNK
    nrow = seq // t
    ntile = S5_W // LANES
    half = S5_TILE_GROUPS * S5_STATE
    per = PROJ_TILE // LANES
    return pl.pallas_call(
        functools.partial(_s5_kernel, nrow=nrow),
        grid=(ntile, b),
        in_specs=[
            pl.BlockSpec((None, seq, LANES), lambda q, bi: (bi, 0, T_SU * per + q)),
            pl.BlockSpec((None, t, LANES, 2 * half), lambda q, bi: (q, 0, 0, 0)),
            pl.BlockSpec((None, t, LANES, LANES), lambda q, bi: (q, 0, 0, 0)),
            pl.BlockSpec((None, t, 2 * half, LANES), lambda q, bi: (q, 0, 0, 0)),
            pl.BlockSpec((None, 1, half), lambda q, bi: (q, 0, 0)),
            pl.BlockSpec((None, 1, half), lambda q, bi: (q, 0, 0)),
        ],
        out_specs=pl.BlockSpec((None, seq, LANES), lambda q, bi: (bi, 0, q)),
        out_shape=jax.ShapeDtypeStruct((b, seq, S5_W), F32),
        scratch_shapes=[pltpu.VMEM((t, nrow, LANES), BF16),
                        pltpu.VMEM((nrow, 2 * half), F32),
                        pltpu.VMEM((nrow, 2 * half), F32)],
        compiler_params=pltpu.CompilerParams(
            dimension_semantics=("parallel", "parallel"),
            vmem_limit_bytes=VMEM_LIMIT_BYTES),
        name="s5",
    )(proj3, w, k, v, a_re, a_im)


def _s5_tables(a_re, a_im, log_dt, b_re, b_im, c_re, c_im):
    t = S5_CHUNK
    hi = lax.Precision.HIGHEST
    step = jnp.exp(log_dt)[:, None]
    k = jnp.arange(t + 1, dtype=F32)[None, :, None]
    mag = jnp.exp(k * (a_re * step)[:, None, :])
    ang = k * (a_im * step)[:, None, :]
    pw_re, pw_im = mag * jnp.cos(ang), mag * jnp.sin(ang)
    ab_re, ab_im = pw_re[:, 1], pw_im[:, 1]
    den = a_re * a_re + a_im * a_im
    n_re, n_im = ab_re - 1.0, ab_im
    cf_re = (n_re * a_re + n_im * a_im) / den
    cf_im = (n_im * a_re - n_re * a_im) / den
    bb_re = cf_re[..., None] * b_re - cf_im[..., None] * b_im
    bb_im = cf_re[..., None] * b_im + cf_im[..., None] * b_re
    cp_re = c_re[:, None] * pw_re[:, :t, None, :] - c_im[:, None] * pw_im[:, :t, None, :]
    cp_im = c_re[:, None] * pw_im[:, :t, None, :] + c_im[:, None] * pw_re[:, :t, None, :]
    kern = (jnp.einsum('gshp,gpk->gshk', cp_re, bb_re, precision=hi)
            - jnp.einsum('gshp,gpk->gshk', cp_im, bb_im, precision=hi))
    g = a_re.shape[0]
    tg = S5_TILE_GROUPS
    nt = g // tg
    eye = jnp.eye(tg, dtype=bool)[None, None, :, None, :, None]

    def tile_diag(m, pattern):
        m5 = jnp.einsum(pattern, m.reshape(nt, tg, t, m.shape[2], m.shape[3]).astype(BF16))
        out = jnp.where(eye, m5[:, :, :, :, None, :], jnp.zeros((), BF16))
        return out.reshape(nt, t, tg * m5.shape[3], tg * m5.shape[4])

    k_bd = tile_diag(kern, 'qgsab->qsgba')
    k_rev = (t - 1) - jnp.arange(t, dtype=F32)[None, :, None]
    mag_r = jnp.exp(k_rev * (a_re * step)[:, None, :])
    ang_r = k_rev * (a_im * step)[:, None, :]
    rp_re, rp_im = mag_r * jnp.cos(ang_r), mag_r * jnp.sin(ang_r)
    bt_re, bt_im = bb_re.transpose(0, 2, 1)[:, None], bb_im.transpose(0, 2, 1)[:, None]
    w_re = rp_re[:, :, None, :] * bt_re - rp_im[:, :, None, :] * bt_im
    w_im = rp_re[:, :, None, :] * bt_im + rp_im[:, :, None, :] * bt_re
    w_bd = jnp.concatenate([tile_diag(w_re, 'qgjhp->qjghp'),
                            tile_diag(w_im, 'qgjhp->qjghp')], axis=-1)
    q_re, q_im = pw_re[:, 1:], pw_im[:, 1:]
    v_re = c_re[:, None] * q_re[:, :, None, :] - c_im[:, None] * q_im[:, :, None, :]
    v_im = c_re[:, None] * q_im[:, :, None, :] + c_im[:, None] * q_re[:, :, None, :]
    v_bd = jnp.concatenate([tile_diag(v_re, 'qgihp->qigph'),
                            tile_diag(-v_im, 'qgihp->qigph')], axis=2)
    at_re = pw_re[:, t].reshape(nt, 1, tg * S5_STATE)
    at_im = pw_im[:, t].reshape(nt, 1, tg * S5_STATE)
    return w_bd.astype(BF16), k_bd.astype(BF16), v_bd.astype(BF16), at_re, at_im


def _outproj_kernel(ym_ref, ys_ref, yw_ref, y5_ref, u_ref, g5_ref, x_ref, w_ref, pg_ref,
                    d5_ref, gw_ref, gb_ref, o_ref):
    y5 = y5_ref[...] + d5_ref[...] * u_ref[...]
    c0 = math.sqrt(2.0 / math.pi)
    y5 = y5 * (0.5 * (1.0 + jnp.tanh(c0 * (y5 + 0.044715 * (y5 * y5 * y5)))))
    gl = jnp.dot(y5.astype(BF16), gw_ref[...], preferred_element_type=F32) + gb_ref[...]
    y5 = y5 * _sigmoid(gl) * _silu(g5_ref[...])
    parts = [ym_ref[...], ys_ref[...], yw_ref[...], y5.astype(BF16)]
    acc = None
    for i, part in enumerate(parts):
        d = jnp.dot(part, w_ref[i * PROJ_TILE:(i + 1) * PROJ_TILE, :], preferred_element_type=F32)
        acc = d if acc is None else acc + d
    ms = jnp.mean(acc * acc, axis=-1, keepdims=True)
    o_ref[...] = x_ref[...] + acc * lax.rsqrt(ms + NORM_EPS) * pg_ref[...]


def _outproj(y_moba, y_ssd, y_swa, y_s5, proj, x2, w_out, post_g, s5_d, glu_w, glu_b, tm):
    m = x2.shape[0]
    rows = lambda w: pl.BlockSpec((tm, w), lambda i: (i, 0))
    full = lambda shape: pl.BlockSpec(shape, lambda i: (0, 0))
    return pl.pallas_call(
        _outproj_kernel,
        grid=(m // tm,),
        in_specs=[
            rows(MOBA_W), rows(SSD_W), rows(SWA_W), rows(S5_W),
            pl.BlockSpec((tm, PROJ_TILE), lambda i: (i, T_SU)),
            pl.BlockSpec((tm, PROJ_TILE), lambda i: (i, T_S5G)),
            rows(D_MODEL),
            full((MIX_W, D_MODEL)),
            full((1, D_MODEL)),
            full((1, S5_W)),
            full((S5_W, S5_W)),
            full((1, S5_W)),
        ],
        out_specs=rows(D_MODEL),
        out_shape=jax.ShapeDtypeStruct((m, D_MODEL), F32),
        compiler_params=pltpu.CompilerParams(
            dimension_semantics=("parallel",),
            vmem_limit_bytes=VMEM_LIMIT_BYTES),
        name="outproj",
    )(y_moba, y_ssd, y_swa, y_s5, proj, proj, x2, w_out, post_g, s5_d, glu_w, glu_b)


def _swa_tiles(w, axis):
    s = w.shape
    per_kv = SWA_HEADS // SWA_KV_HEADS
    w = w.reshape(s[:axis] + (SWA_KV_HEADS, per_kv, HEAD_DIM) + s[axis + 1:])
    return jnp.swapaxes(w, axis, axis + 1).reshape(s)


def _rotary_pairs(w):
    s = w.shape
    w = w.reshape(s[:-1] + (s[-1] // LANES, 2, 2, HEAD_DIM // 2))
    return jnp.swapaxes(w, -2, -3).reshape(s)


def _rope_tables(seq):
    inv = 1.0 / (ROPE_THETA ** (jnp.arange(0, HEAD_DIM, 2, dtype=F32) / HEAD_DIM))
    ang = jnp.arange(seq, dtype=F32)[:, None] * inv[None, :]
    cos, sin = jnp.cos(ang), jnp.sin(ang)
    cos_t = jnp.concatenate([cos, cos, cos, cos], axis=1)
    sin_t = jnp.concatenate([-sin, -sin, sin, sin], axis=1)
    reps = PROJ_TILE // LANES
    return jnp.tile(cos_t, (1, reps)), jnp.tile(sin_t, (1, reps))


def _cat_w_in(w_in):
    offs = np.cumsum([0, MOBA_W, MOBA_W, MOBA_W, MOBA_W, SSD_CONV_CH, SSD_HEADS, SSD_W,
                      SWA_W, SWA_KV_W, SWA_KV_W, SWA_W, S5_W, S5_W])
    (o_mq, o_mk, o_mv, o_mg, o_xbc, o_dt, o_z, o_sq, o_sk, o_sv, o_sg, o_su, o_s5g, _) = [int(v) for v in offs]
    sl = lambda o, w: w_in[..., o:o + w]
    pad = jnp.zeros(w_in.shape[:-1] + (PROJ_TILE - 2 * SWA_KV_W - SSD_HEADS,), w_in.dtype)
    pieces = [_rotary_pairs(sl(o_mq, MOBA_W)), _rotary_pairs(sl(o_mk, MOBA_W)),
              sl(o_mv, MOBA_W), sl(o_mg, MOBA_W),
              sl(o_xbc, SSD_CONV_CH), sl(o_z, SSD_W),
              _rotary_pairs(_swa_tiles(sl(o_sq, SWA_W), 2)), _swa_tiles(sl(o_sg, SWA_W), 2),
              sl(o_su, S5_W), sl(o_s5g, S5_W),
              _rotary_pairs(sl(o_sk, SWA_KV_W)), sl(o_sv, SWA_KV_W), sl(o_dt, SSD_HEADS), pad]
    return jnp.concatenate(pieces, axis=-1).astype(BF16)


def _pad_lanes(v):
    return jnp.pad(v, (0, LANES - v.shape[0]))[None, :]


def _layer(x2, b, seq, cos_t, sin_t, pre_g, post_g, w_in, w_out, conv_w, conv_b, dt_bias, a_log,
           ssd_d, ssd_norm, sinks, s5_tabs, s5_d, glu_w, glu_b):
    proj = _inproj(x2, pre_g[None, :], w_in, cos_t, sin_t, seq, min(TM_INPROJ, seq))
    proj3 = proj.reshape(b, seq, PROJ_W)

    y_moba = _moba(proj3)
    y_swa = _swa(proj3, sinks)
    y_ssd = _ssd(proj3, conv_w, conv_b[None, :], _pad_lanes(dt_bias), _pad_lanes(a_log),
                 jnp.repeat(ssd_d, SSD_HEAD_DIM)[None, :], ssd_norm[None, :])
    y_s5 = _s5_core(proj3, *s5_tabs).reshape(b * seq, S5_W)

    m = b * seq
    return _outproj(y_moba.reshape(m, MOBA_W), y_ssd.reshape(m, SSD_W), y_swa.reshape(m, SWA_W),
                    y_s5, proj, x2, w_out, post_g[None, :], s5_d[None, :],
                    glu_w, glu_b[None, :], min(TM_OUTPROJ, seq))


def kernel(x, pre_norm, post_norm, w_in, w_out, ssd_conv_w, ssd_conv_b, ssd_dt_bias, ssd_a_log, ssd_d, ssd_norm, swa_sinks, s5_a_re, s5_a_im, s5_log_dt, s5_b_re, s5_b_im, s5_c_re, s5_c_im, s5_d, s5_glu_w, s5_glu_b):
    b, seq, d = x.shape
    assert d == D_MODEL
    assert seq % MOBA_BLOCK == 0 and seq % SSD_CHUNK == 0 and seq % SWA_WINDOW == 0
    assert seq % (S5_CHUNK * S5_SCAN_ROWS) == 0
    assert seq % SWA_TQ == 0
    depth = pre_norm.shape[0]
    cos_t, sin_t = _rope_tables(seq)
    w_cat = _cat_w_in(w_in)
    swa_lo = MOBA_W + SSD_W
    w_out_p = jnp.concatenate([w_out[:, :swa_lo], _swa_tiles(w_out[:, swa_lo:swa_lo + SWA_W], 1),
                               w_out[:, swa_lo + SWA_W:]], axis=1).astype(BF16)
    per_kv = SWA_HEADS // SWA_KV_HEADS
    sinks_p = jnp.swapaxes(swa_sinks.reshape(depth, SWA_KV_HEADS, per_kv), 1, 2).reshape(depth, SWA_HEADS)
    glu_w16 = s5_glu_w.astype(BF16)
    fold = lambda a: a.reshape((depth * S5_GROUPS,) + a.shape[2:])
    tabs = _s5_tables(fold(s5_a_re), fold(s5_a_im), s5_log_dt.reshape(-1), fold(s5_b_re), fold(s5_b_im),
                      fold(s5_c_re), fold(s5_c_im))
    tabs = [a.reshape((depth, a.shape[0] // depth) + a.shape[1:]) for a in tabs]

    x2 = x.reshape(b * seq, d)
    for l in range(depth):
        x2 = _layer(x2, b, seq, cos_t, sin_t, pre_norm[l], post_norm[l], w_cat[l], w_out_p[l],
                    ssd_conv_w[l], ssd_conv_b[l], ssd_dt_bias[l], ssd_a_log[l], ssd_d[l], ssd_norm[l],
                    sinks_p[l], [a[l] for a in tabs], s5_d[l], glu_w16[l], s5_glu_b[l])
    return x2.reshape(b, seq, d)
```

```python
import functools
import math

import jax
import jax.numpy as jnp
import numpy as np
from jax import lax
from jax.experimental import pallas as pl
from jax.experimental.pallas import tpu as pltpu

F32 = jnp.float32
BF16 = jnp.bfloat16

D_MODEL = 2048
HEAD_DIM = 64
ROPE_THETA = 10000.0
NORM_EPS = 1e-6
MOBA_HEADS = 8
MOBA_W = MOBA_HEADS * HEAD_DIM
MOBA_BLOCK = 256
MOBA_TOPK = 3
SSD_HEADS = 8
SSD_HEAD_DIM = 64
SSD_W = SSD_HEADS * SSD_HEAD_DIM
SSD_GROUPS = 2
SSD_STATE = 128
SSD_CONV = 4
SSD_CHUNK = 256
SSD_CONV_CH = SSD_W + 2 * SSD_GROUPS * SSD_STATE
SWA_HEADS = 8
SWA_KV_HEADS = 2
SWA_W = SWA_HEADS * HEAD_DIM
SWA_KV_W = SWA_KV_HEADS * HEAD_DIM
SWA_WINDOW = 128
S5_W = 512
S5_GROUP = 16
S5_GROUPS = S5_W // S5_GROUP
S5_STATE = 64
MIX_W = MOBA_W + SSD_W + SWA_W + S5_W

LANES = 128
SUBLANES = 8
VMEM_LIMIT_BYTES = 48 * 1024 * 1024

PROJ_TILE = 512
T_MQ, T_MK, T_MV, T_MG, T_XBC, T_Z, T_SQ, T_SG, T_SU, T_S5G, T_TAIL = 0, 1, 2, 3, 4, 6, 7, 8, 9, 10, 11
PROJ_W = 12 * PROJ_TILE
TAIL_K_BLK = T_TAIL * (PROJ_TILE // LANES)
TAIL_V_BLK = TAIL_K_BLK + 1
TAIL_DT_BLK = TAIL_K_BLK + 2

NEG = -1e30
TM_INPROJ = 1024
TM_OUTPROJ = 512
SWA_TQ = 512
MOBA_PAIRS = 4
MOBA_ONES_ROWS = 16
S5_CHUNK = 8
S5_TILE_GROUPS = LANES // S5_GROUP
S5_SCAN_ROWS = SUBLANES


def _silu(x):
    return x / (1.0 + jnp.exp(-x))


def _sigmoid(x):
    return 1.0 / (1.0 + jnp.exp(-x))


def _bf16_parts(x):
    hi = x.astype(BF16)
    r1 = x - hi.astype(F32)
    mid = r1.astype(BF16)
    lo = (r1 - mid.astype(F32)).astype(BF16)
    return hi, mid, lo


def _nt_dot(a, b, precision=None):
    return lax.dot_general(a, b, (((1,), (1,)), ((), ())), precision=precision,
                           preferred_element_type=F32)


def _tn_dot(a, b, precision=None):
    return lax.dot_general(a, b, (((0,), (0,)), ((), ())), precision=precision,
                           preferred_element_type=F32)


def _inproj_kernel(x_ref, g_ref, w_ref, cos_ref, sin_ref, o_ref, h_scr):
    j = pl.program_id(1)

    @pl.when(j == 0)
    def _():
        x = x_ref[...]
        ms = jnp.mean(x * x, axis=-1, keepdims=True)
        h_scr[...] = (x * lax.rsqrt(ms + NORM_EPS) * g_ref[...]).astype(BF16)

    acc = jnp.dot(h_scr[...], w_ref[...], preferred_element_type=F32)
    is_rope = (j == T_MQ) | (j == T_MK) | (j == T_SQ) | (j == T_TAIL)

    @pl.when(is_rope)
    def _():
        rot = jnp.concatenate(
            [pltpu.roll(acc[:, t * LANES:(t + 1) * LANES], LANES // 2, axis=1)
             for t in range(acc.shape[1] // LANES)], axis=1)
        roped = acc * cos_ref[...] + rot * sin_ref[...]
        lane = lax.broadcasted_iota(jnp.int32, acc.shape, 1)
        apply = (j != T_TAIL) | (lane < SWA_KV_W)
        o_ref[...] = jnp.where(apply, roped, acc)

    @pl.when(jnp.logical_not(is_rope))
    def _():
        o_ref[...] = acc


def _inproj(x2, pre_g, w_cat, layer, cos_t, sin_t, seq, tm):
    m = x2.shape[0]
    nseq = seq // tm
    return pl.pallas_call(
        _inproj_kernel,
        grid=(m // tm, PROJ_W // PROJ_TILE),
        in_specs=[
            pl.BlockSpec((tm, D_MODEL), lambda i, j: (i, 0)),
            pl.BlockSpec((1, D_MODEL), lambda i, j: (0, 0)),
            pl.BlockSpec((None, D_MODEL, PROJ_TILE), lambda i, j: (layer, 0, j)),
            pl.BlockSpec((tm, PROJ_TILE), lambda i, j: (i % nseq, 0)),
            pl.BlockSpec((tm, PROJ_TILE), lambda i, j: (i % nseq, 0)),
        ],
        out_specs=pl.BlockSpec((tm, PROJ_TILE), lambda i, j: (i, j)),
        out_shape=jax.ShapeDtypeStruct((m, PROJ_W), F32),
        scratch_shapes=[pltpu.VMEM((tm, D_MODEL), BF16)],
        compiler_params=pltpu.CompilerParams(
            dimension_semantics=("parallel", "arbitrary"),
            vmem_limit_bytes=VMEM_LIMIT_BYTES),
        name="inproj",
    )(x2, pre_g, w_cat, cos_t, sin_t)


def _pair_head(idx):
    return (idx // (HEAD_DIM // 2)) % 2


def _moba_kernel(q_ref, k_ref, v_ref, g_ref, o_ref, kaug_scr, vt_scr, kmean_scr, qaug_scr,
                 acc_scr, sa_scr, sb_scr, *, nb):
    qi = pl.program_id(2)
    blk = MOBA_BLOCK
    qscale = (HEAD_DIM ** -0.5) * math.log2(math.e)

    nh = 2 * MOBA_PAIRS

    @pl.when(qi == 0)
    def _():
        seq = k_ref.shape[0]
        row_blk = lax.broadcasted_iota(jnp.int32, (seq, LANES), 0) // blk
        lane = lax.broadcasted_iota(jnp.int32, (seq, LANES), 1)
        onehot = jnp.where(row_blk == lane, 1.0, 0.0).astype(BF16)
        kmean_scr[...] = jnp.mean(k_ref[...].reshape(nb, blk, MOBA_PAIRS * LANES), axis=1)
        for pp in range(MOBA_PAIRS):
            kaug_scr[pp, :, :LANES] = k_ref[:, pp * LANES:(pp + 1) * LANES].astype(BF16)
            kaug_scr[pp, :, LANES:] = onehot
        ones = jnp.ones((MOBA_ONES_ROWS, blk), BF16)
        for n in range(nb):
            vt = v_ref[n * blk:(n + 1) * blk, :].T.astype(BF16)
            for hd in range(nh):
                vt_scr[hd, n, 0:HEAD_DIM, :] = vt[hd * HEAD_DIM:(hd + 1) * HEAD_DIM, :]
                vt_scr[hd, n, HEAD_DIM:, :] = ones

    q_all = q_ref[...].T
    sub = lax.broadcasted_iota(jnp.int32, (LANES, blk), 0)
    klane = lax.broadcasted_iota(jnp.int32, (nb, LANES), 1)
    blk_id = lax.broadcasted_iota(jnp.int32, (nb, blk), 0)
    past = blk_id < qi

    for hd in range(nh):
        pp, hh = divmod(hd, 2)
        q_t = q_all[pp * LANES:(pp + 1) * LANES]
        km = jnp.where(_pair_head(klane) == hh, kmean_scr[:, pp * LANES:(pp + 1) * LANES], 0.0)
        gate = jnp.dot(km, q_t, precision=lax.Precision.HIGHEST, preferred_element_type=F32)
        gm = jnp.where(past, gate, -jnp.inf)
        cnt = jnp.zeros((nb, blk), F32)
        for n in range(nb):
            gn = gm[n:n + 1, :]
            beats = (gn > gm) | ((gn == gm) & (blk_id > n))
            cnt = cnt + jnp.where(beats & (qi > n), 1.0, 0.0)
        keep = (past & (cnt < float(MOBA_TOPK))) | (blk_id == qi)
        pen = jnp.where(keep, 0.0, NEG)
        pen = jnp.concatenate([pen, jnp.zeros((LANES - nb, blk), F32)], axis=0)
        qh = jnp.where(_pair_head(sub) == hh, q_t, 0.0) * qscale
        qaug_scr[hd] = jnp.concatenate([qh.astype(BF16), pen.astype(BF16)], axis=0)
        acc_scr[hd] = jnp.zeros(acc_scr.shape[1:], F32)

    def qk(t, s_ref):
        start = pl.multiple_of(t * blk, blk)
        for pp in range(MOBA_PAIRS):
            kb = kaug_scr[pp, pl.ds(start, blk), :]
            for hd in (2 * pp, 2 * pp + 1):
                s_ref[hd] = jnp.dot(kb, qaug_scr[hd], preferred_element_type=F32)

    def process(t, s_ref, ms, tail):
        out = []
        for hd in range(nh):
            s = s_ref[hd]
            if tail:
                s = jnp.where(key_minus_qry <= (qi - t) * blk, s, NEG)
            m_new = jnp.maximum(ms[hd], jnp.max(s, axis=0, keepdims=True))
            alpha = jnp.exp2(ms[hd] - m_new)
            p = jnp.exp2(s - m_new).astype(BF16)
            pv = jnp.dot(vt_scr[hd, jnp.minimum(t, qi)], p, preferred_element_type=F32)
            acc_scr[hd] = alpha * acc_scr[hd] + pv
            out.append(m_new)
        return tuple(out)

    qk(0, sa_scr)

    def body(k, ms):
        t = 2 * k
        qk(t + 1, sb_scr)
        ms = process(t, sa_scr, ms, False)
        qk(t + 2, sa_scr)
        return process(t + 1, sb_scr, ms, False)

    m_init = jnp.full((1, blk), NEG, F32)
    ms = lax.fori_loop(0, qi // 2, body, (m_init,) * nh)
    key_minus_qry = (lax.broadcasted_iota(jnp.int32, (blk, blk), 0)
                     - lax.broadcasted_iota(jnp.int32, (blk, blk), 1))
    t1 = 2 * (qi // 2)
    has_second = t1 < qi

    @pl.when(has_second)
    def _():
        qk(t1 + 1, sb_scr)
        process(t1 + 1, sb_scr, process(t1, sa_scr, ms, False), True)

    @pl.when(jnp.logical_not(has_second))
    def _():
        process(t1, sa_scr, ms, True)

    outs = [acc_scr[hd][:HEAD_DIM] / acc_scr[hd][HEAD_DIM:HEAD_DIM + 1] for hd in range(nh)]
    out = jnp.concatenate(outs, axis=0).T
    o_ref[...] = (out * _silu(g_ref[...])).astype(o_ref.dtype)


def _moba(proj3):
    b, seq, _ = proj3.shape
    blk = MOBA_BLOCK
    nb = seq // blk
    wid = MOBA_PAIRS * LANES
    ngrp = MOBA_W // wid
    per = PROJ_TILE // wid
    nh = 2 * MOBA_PAIRS
    return pl.pallas_call(
        functools.partial(_moba_kernel, nb=nb),
        grid=(b, ngrp, nb),
        in_specs=[
            pl.BlockSpec((None, blk, wid), lambda bi, hp, qi: (bi, qi, T_MQ * per + hp)),
            pl.BlockSpec((None, seq, wid), lambda bi, hp, qi: (bi, 0, T_MK * per + hp),
                         pipeline_mode=pl.Buffered(1)),
            pl.BlockSpec((None, seq, wid), lambda bi, hp, qi: (bi, 0, T_MV * per + hp),
                         pipeline_mode=pl.Buffered(1)),
            pl.BlockSpec((None, blk, wid), lambda bi, hp, qi: (bi, qi, T_MG * per + hp)),
        ],
        out_specs=pl.BlockSpec((None, blk, wid), lambda bi, hp, qi: (bi, qi, hp)),
        out_shape=jax.ShapeDtypeStruct((b, seq, MOBA_W), BF16),
        scratch_shapes=[pltpu.VMEM((MOBA_PAIRS, seq, 2 * LANES), BF16),
                        pltpu.VMEM((nh, nb, HEAD_DIM + MOBA_ONES_ROWS, blk), BF16),
                        pltpu.VMEM((nb, wid), F32),
                        pltpu.VMEM((nh, 2 * LANES, blk), BF16),
                        pltpu.VMEM((nh, HEAD_DIM + MOBA_ONES_ROWS, blk), F32),
                        pltpu.VMEM((nh, blk, blk), F32),
                        pltpu.VMEM((nh, blk, blk), F32)],
        compiler_params=pltpu.CompilerParams(
            dimension_semantics=("parallel", "parallel", "arbitrary"),
            vmem_limit_bytes=VMEM_LIMIT_BYTES),
        name="moba",
    )(proj3, proj3, proj3, proj3)


def _swa_kernel(sink_ref, q_ref, kp_ref, kc_ref, vp_ref, vc_ref, g_ref, o_ref):
    n = pl.program_id(1)
    w = SWA_WINDOW
    scale = HEAD_DIM ** -0.5
    kcat = jnp.concatenate([kp_ref[...], kc_ref[...]], axis=0).astype(BF16)
    vcat = jnp.concatenate([vp_ref[...], vc_ref[...]], axis=0).astype(BF16)
    row = lax.broadcasted_iota(jnp.int32, (w, 2 * w), 0)
    col = lax.broadcasted_iota(jnp.int32, (w, 2 * w), 1)
    band = (col > row) & (col <= row + w)
    lane = lax.broadcasted_iota(jnp.int32, (w, LANES), 1)
    for win in range(SWA_TQ // w):
        valid = band & ((n > 0) | (col >= w)) if win == 0 else band
        kw = kcat[win * w:(win + 2) * w]
        vw = vcat[win * w:(win + 2) * w]
        rows = slice(win * w, (win + 1) * w)
        for i in range(SWA_W // LANES):
            q_t = q_ref[rows, i * LANES:(i + 1) * LANES]
            halves = []
            for c in range(2):
                qm = (jnp.where(_pair_head(lane) == c, q_t, 0.0) * scale).astype(BF16)
                s = _nt_dot(qm, kw)
                s = jnp.where(valid, s, NEG)
                sink = sink_ref[2 * i + c]
                m = jnp.maximum(jnp.max(s, axis=-1, keepdims=True), sink)
                e = jnp.exp(s - m)
                denom = jnp.sum(e, axis=-1, keepdims=True) + jnp.exp(sink - m)
                o = jnp.dot(e.astype(BF16), vw, preferred_element_type=F32)
                halves.append(o / denom)
            out = jnp.where(lane < HEAD_DIM, halves[0], halves[1])
            g_t = g_ref[rows, i * LANES:(i + 1) * LANES]
            o_ref[rows, i * LANES:(i + 1) * LANES] = (out * _silu(g_t)).astype(o_ref.dtype)


def _swa(proj3, sinks_perm):
    b, seq, _ = proj3.shape
    w = SWA_WINDOW
    tq = SWA_TQ
    per = tq // w
    prev = lambda bi, n: (bi, jnp.maximum(n * per - 1, 0), TAIL_K_BLK)
    prev_v = lambda bi, n: (bi, jnp.maximum(n * per - 1, 0), TAIL_V_BLK)
    return pl.pallas_call(
        _swa_kernel,
        grid=(b, seq // tq),
        in_specs=[
            pl.BlockSpec(memory_space=pltpu.SMEM),
            pl.BlockSpec((None, tq, SWA_W), lambda bi, n: (bi, n, T_SQ)),
            pl.BlockSpec((None, w, LANES), prev),
            pl.BlockSpec((None, tq, LANES), lambda bi, n: (bi, n, TAIL_K_BLK)),
            pl.BlockSpec((None, w, LANES), prev_v),
            pl.BlockSpec((None, tq, LANES), lambda bi, n: (bi, n, TAIL_V_BLK)),
            pl.BlockSpec((None, tq, SWA_W), lambda bi, n: (bi, n, T_SG)),
        ],
        out_specs=pl.BlockSpec((None, tq, SWA_W), lambda bi, n: (bi, n, 0)),
        out_shape=jax.ShapeDtypeStruct((b, seq, SWA_W), BF16),
        compiler_params=pltpu.CompilerParams(
            dimension_semantics=("parallel", "arbitrary"),
            vmem_limit_bytes=VMEM_LIMIT_BYTES),
        name="swa",
    )(sinks_perm, proj3, proj3, proj3, proj3, proj3, proj3)


def _ssd_kernel(xbc_ref, z_ref, dt_ref, cw_ref, cb_ref, dtb_ref, alog_ref, dsk_ref, nw_ref,
                o_ref, xbuf, state):
    c = pl.program_id(1)
    s = SSD_CHUNK
    pad = SUBLANES
    hi = lax.Precision.HIGHEST

    @pl.when(c == 0)
    def _():
        xbuf[0:pad, :] = jnp.zeros((pad, SSD_CONV_CH), F32)
        state[...] = jnp.zeros_like(state)

    @pl.when(c > 0)
    def _():
        xbuf[0:pad, :] = xbuf[s:s + pad, :]

    xbuf[pad:pad + s, :] = xbc_ref[...]
    conv = cb_ref[...]
    for k in range(SSD_CONV):
        off = pad - (SSD_CONV - 1) + k
        conv = conv + cw_ref[k:k + 1, :] * xbuf[off:off + s, :]
    xbc = _silu(conv)
    xs = xbc[:, :SSD_W]
    gw = SSD_GROUPS * SSD_STATE
    bm = xbc[:, SSD_W:SSD_W + gw]
    cm = xbc[:, SSD_W + gw:]

    dtr = dt_ref[...] + dtb_ref[...]
    dt = jnp.maximum(dtr, 0.0) + jnp.log(1.0 + jnp.exp(-jnp.abs(dtr)))
    a = dt * (-jnp.exp(alog_ref[...]))
    row = lax.broadcasted_iota(jnp.int32, (s, s), 0)
    col = lax.broadcasted_iota(jnp.int32, (s, s), 1)
    lower = row >= col
    tri = jnp.where(lower, 1.0, 0.0).astype(BF16)
    acum = sum(jnp.dot(tri, part, preferred_element_type=F32) for part in _bf16_parts(a))
    acum_t = acum.T
    alast = acum[s - 1:s, :]

    erow = lax.broadcasted_iota(jnp.int32, (LANES, SSD_W), 0)
    ecol = lax.broadcasted_iota(jnp.int32, (LANES, SSD_W), 1)
    expand = jnp.where(erow == ecol // SSD_HEAD_DIM, 1.0, 0.0).astype(BF16)
    cols = jnp.concatenate([dt, jnp.exp(alast - acum), jnp.exp(acum),
                            jnp.broadcast_to(jnp.exp(alast), (SUBLANES, LANES))], axis=0)
    cols_e = sum(jnp.dot(part, expand, preferred_element_type=F32) for part in _bf16_parts(cols))
    dt_e, dec_e, ea_e = cols_e[:s], cols_e[s:2 * s], cols_e[2 * s:3 * s]
    cd_e = cols_e[3 * s:3 * s + 1]

    x_dt = xs * dt_e
    x_dec = (x_dt * dec_e).astype(BF16)
    x_dt16 = x_dt.astype(BF16)
    lane = lax.broadcasted_iota(jnp.int32, (s, LANES), 1)
    hpg = SSD_HEADS // SSD_GROUPS
    gwid = hpg * SSD_HEAD_DIM
    y_parts = []
    for g in range(SSD_GROUPS):
        bg = bm[:, g * SSD_STATE:(g + 1) * SSD_STATE].astype(BF16)
        cg = cm[:, g * SSD_STATE:(g + 1) * SSD_STATE].astype(BF16)
        cbm = _nt_dot(cg, bg)
        st = state[:, g * gwid:(g + 1) * gwid]
        y_off = jnp.dot(cg, st.astype(BF16), preferred_element_type=F32) \
            * ea_e[:, g * gwid:(g + 1) * gwid]
        new_st = cd_e[:, g * gwid:(g + 1) * gwid] * st + _tn_dot(bg, x_dec[:, g * gwid:(g + 1) * gwid])
        state[:, g * gwid:(g + 1) * gwid] = new_st
        for i in range(hpg // 2):
            lo = g * gwid + i * LANES
            xp = x_dt16[:, lo:lo + LANES]
            ys = []
            for hh in range(2):
                h = g * hpg + 2 * i + hh
                seg = acum[:, h:h + 1] - acum_t[h:h + 1, :]
                lm = jnp.where(lower, jnp.exp(jnp.minimum(seg, 0.0)), 0.0)
                ys.append(jnp.dot((cbm * lm).astype(BF16), xp, preferred_element_type=F32))
            y_parts.append(jnp.where(lane < SSD_HEAD_DIM, ys[0], ys[1])
                           + y_off[:, i * LANES:(i + 1) * LANES])
    y = jnp.concatenate(y_parts, axis=1)
    y = (y + dsk_ref[...] * xs) * _silu(z_ref[...])
    outs = []
    for g in range(SSD_GROUPS):
        yg = y[:, g * gwid:(g + 1) * gwid]
        outs.append(yg * lax.rsqrt(jnp.mean(yg * yg, axis=-1, keepdims=True) + NORM_EPS))
    o_ref[...] = (jnp.concatenate(outs, axis=1) * nw_ref[...]).astype(o_ref.dtype)


def _ssd(proj3, conv_w, conv_b, dt_bias_p, a_log_p, d_skip_e, norm_w):
    b, seq, _ = proj3.shape
    s = SSD_CHUNK
    full = lambda shape: pl.BlockSpec(shape, lambda bi, c: (0, 0))
    return pl.pallas_call(
        _ssd_kernel,
        grid=(b, seq // s),
        in_specs=[
            pl.BlockSpec((None, s, SSD_CONV_CH), lambda bi, c: (bi, c, T_XBC * PROJ_TILE // SSD_CONV_CH)),
            pl.BlockSpec((None, s, SSD_W), lambda bi, c: (bi, c, T_Z)),
            pl.BlockSpec((None, s, LANES), lambda bi, c: (bi, c, TAIL_DT_BLK)),
            full((SSD_CONV, SSD_CONV_CH)),
            full((1, SSD_CONV_CH)),
            full((1, LANES)),
            full((1, LANES)),
            full((1, SSD_W)),
            full((1, SSD_W)),
        ],
        out_specs=pl.BlockSpec((None, s, SSD_W), lambda bi, c: (bi, c, 0)),
        out_shape=jax.ShapeDtypeStruct((b, seq, SSD_W), BF16),
        scratch_shapes=[pltpu.VMEM((s + 2 * SUBLANES, SSD_CONV_CH), F32),
                        pltpu.VMEM((SSD_STATE, SSD_W), F32)],
        compiler_params=pltpu.CompilerParams(
            dimension_semantics=("parallel", "arbitrary"),
            vmem_limit_bytes=VMEM_LIMIT_BYTES),
        name="ssd",
    )(proj3, proj3, proj3, conv_w, conv_b, dt_bias_p, a_log_p, d_skip_e, norm_w)


def _s5_kernel(u_ref, w_ref, k_ref, v_ref, are_ref, aim_ref, y_ref, x_scr, e_scr, s_scr, y_scr,
               *, nrow):
    t = S5_CHUNK
    half = S5_TILE_GROUPS * S5_STATE
    for j in range(t):
        x_scr[j // 2, :, (j % 2) * LANES:(j % 2 + 1) * LANES] = (
            u_ref[pl.ds(j, nrow, stride=t), :].astype(BF16))
    e = jnp.dot(x_scr[0], w_ref[0], preferred_element_type=F32)
    for jp in range(1, t // 2):
        e = e + jnp.dot(x_scr[jp], w_ref[jp], preferred_element_type=F32)
    e_scr[...] = e
    a_re = are_ref[...]
    a_im = aim_ref[...]

    def body(it, carry):
        s_re, s_im = carry
        r0 = pl.multiple_of(it * S5_SCAN_ROWS, S5_SCAN_ROWS)
        e8 = e_scr[pl.ds(r0, S5_SCAN_ROWS), :]
        rows = []
        for kk in range(S5_SCAN_ROWS):
            rows.append(jnp.concatenate([s_re, s_im], axis=1))
            e_re, e_im = e8[kk:kk + 1, :half], e8[kk:kk + 1, half:]
            s_re, s_im = a_re * s_re - a_im * s_im + e_re, a_re * s_im + a_im * s_re + e_im
        s_scr[pl.ds(r0, S5_SCAN_ROWS), :] = jnp.concatenate(rows, axis=0)
        return s_re, s_im

    zero = jnp.zeros((1, half), F32)
    lax.fori_loop(0, nrow // S5_SCAN_ROWS, body, (zero, zero))
    y_scr[...] = jnp.dot(s_scr[...].astype(BF16), v_ref[...], preferred_element_type=F32)
    for jp in range(t // 2):
        j = 2 * jp
        y_scr[:, j * LANES:] += jnp.dot(x_scr[jp], k_ref[:, :(t - j) * LANES],
                                        preferred_element_type=F32)
    for i in range(t):
        y_ref[pl.ds(i, nrow, stride=t), :] = y_scr[:, i * LANES:(i + 1) * LANES]


def _s5_core(proj3, w, k, v, a_re, a_im, layer):
    b, seq, _ = proj3.shape
    t = S5_CHUNK
    nrow = seq // t
    ntile = S5_W // LANES
    half = S5_TILE_GROUPS * S5_STATE
    per = PROJ_TILE // LANES
    base = layer * ntile
    return pl.pallas_call(
        functools.partial(_s5_kernel, nrow=nrow),
        grid=(ntile, b),
        in_specs=[
            pl.BlockSpec((None, seq, LANES), lambda q, bi: (bi, 0, T_SU * per + q)),
            pl.BlockSpec((None, t // 2, 2 * LANES, 2 * half), lambda q, bi: (base + q, 0, 0, 0)),
            pl.BlockSpec((None, 2 * LANES, t * LANES), lambda q, bi: (base + q, 0, 0)),
            pl.BlockSpec((None, 2 * half, t * LANES), lambda q, bi: (base + q, 0, 0)),
            pl.BlockSpec((None, 1, half), lambda q, bi: (base + q, 0, 0)),
            pl.BlockSpec((None, 1, half), lambda q, bi: (base + q, 0, 0)),
        ],
        out_specs=pl.BlockSpec((None, seq, LANES), lambda q, bi: (bi, 0, q)),
        out_shape=jax.ShapeDtypeStruct((b, seq, S5_W), F32),
        scratch_shapes=[pltpu.VMEM((t // 2, nrow, 2 * LANES), BF16),
                        pltpu.VMEM((nrow, 2 * half), F32),
                        pltpu.VMEM((nrow, 2 * half), F32),
                        pltpu.VMEM((nrow, t * LANES), F32)],
        compiler_params=pltpu.CompilerParams(
            dimension_semantics=("parallel", "parallel"),
            vmem_limit_bytes=VMEM_LIMIT_BYTES),
        name="s5",
    )(proj3, w, k, v, a_re, a_im)


def _s5_tables(a_re, a_im, log_dt, b_re, b_im, c_re, c_im):
    t = S5_CHUNK
    hi = lax.Precision.HIGHEST
    step = jnp.exp(log_dt)[:, None]
    k = jnp.arange(t + 1, dtype=F32)[None, :, None]
    mag = jnp.exp(k * (a_re * step)[:, None, :])
    ang = k * (a_im * step)[:, None, :]
    pw_re, pw_im = mag * jnp.cos(ang), mag * jnp.sin(ang)
    ab_re, ab_im = pw_re[:, 1], pw_im[:, 1]
    den = a_re * a_re + a_im * a_im
    n_re, n_im = ab_re - 1.0, ab_im
    cf_re = (n_re * a_re + n_im * a_im) / den
    cf_im = (n_im * a_re - n_re * a_im) / den
    bb_re = cf_re[..., None] * b_re - cf_im[..., None] * b_im
    bb_im = cf_re[..., None] * b_im + cf_im[..., None] * b_re
    cp_re = c_re[:, None] * pw_re[:, :t, None, :] - c_im[:, None] * pw_im[:, :t, None, :]
    cp_im = c_re[:, None] * pw_im[:, :t, None, :] + c_im[:, None] * pw_re[:, :t, None, :]
    kern = (jnp.einsum('gshp,gpk->gshk', cp_re, bb_re, precision=hi)
            - jnp.einsum('gshp,gpk->gshk', cp_im, bb_im, precision=hi))
    g = a_re.shape[0]
    tg = S5_TILE_GROUPS
    nt = g // tg
    def tile_diag(m, pattern):
        m5 = jnp.einsum(pattern, m.reshape(nt, tg, t, m.shape[2], m.shape[3]).astype(BF16))
        r, c = m5.shape[3], m5.shape[4]
        rep = (jnp.arange(c)[:, None] == jnp.arange(tg * c)[None, :] % c).astype(BF16)
        out = jnp.einsum('qsrc,cn->qsrn', m5.reshape(nt, t, tg * r, c), rep,
                         preferred_element_type=BF16)
        on_diag = (jnp.arange(tg * r)[:, None] // r) == (jnp.arange(tg * c)[None, :] // c)
        return jnp.where(on_diag, out, jnp.zeros((), BF16))

    k_bd = tile_diag(kern, 'qgsab->qsgba')
    k_rev = (t - 1) - jnp.arange(t, dtype=F32)[None, :, None]
    mag_r = jnp.exp(k_rev * (a_re * step)[:, None, :])
    ang_r = k_rev * (a_im * step)[:, None, :]
    rp_re, rp_im = mag_r * jnp.cos(ang_r), mag_r * jnp.sin(ang_r)
    bt_re, bt_im = bb_re.transpose(0, 2, 1)[:, None], bb_im.transpose(0, 2, 1)[:, None]
    w_re = rp_re[:, :, None, :] * bt_re - rp_im[:, :, None, :] * bt_im
    w_im = rp_re[:, :, None, :] * bt_im + rp_im[:, :, None, :] * bt_re
    w_bd = jnp.concatenate([tile_diag(w_re, 'qgjhp->qjghp'),
                            tile_diag(w_im, 'qgjhp->qjghp')], axis=-1)
    q_re, q_im = pw_re[:, 1:], pw_im[:, 1:]
    v_re = c_re[:, None] * q_re[:, :, None, :] - c_im[:, None] * q_im[:, :, None, :]
    v_im = c_re[:, None] * q_im[:, :, None, :] + c_im[:, None] * q_re[:, :, None, :]
    v_bd = jnp.concatenate([tile_diag(v_re, 'qgihp->qigph'),
                            tile_diag(-v_im, 'qgihp->qigph')], axis=2)
    at_re = pw_re[:, t].reshape(nt, 1, tg * S5_STATE)
    at_im = pw_im[:, t].reshape(nt, 1, tg * S5_STATE)
    w_pair = w_bd.reshape(nt, t // 2, 2 * LANES, w_bd.shape[-1])
    k_all = k_bd.transpose(0, 2, 1, 3).reshape(nt, LANES, t * LANES)
    k_lag = jnp.concatenate([jnp.zeros((nt, LANES, LANES), BF16), k_all[:, :, :-LANES]], axis=2)
    k_pair = jnp.concatenate([k_all, k_lag], axis=1)
    v_all = v_bd.transpose(0, 2, 1, 3).reshape(nt, v_bd.shape[2], t * LANES)
    return w_pair, k_pair, v_all, at_re, at_im


def _outproj_kernel(ym_ref, ys_ref, yw_ref, y5_ref, u_ref, g5_ref, x_ref, w_ref, pg_ref,
                    d5_ref, gw_ref, gb_ref, o_ref):
    y5 = y5_ref[...] + d5_ref[...] * u_ref[...]
    c0 = math.sqrt(2.0 / math.pi)
    y5 = y5 * (0.5 * (1.0 + jnp.tanh(c0 * (y5 + 0.044715 * (y5 * y5 * y5)))))
    gl = jnp.dot(y5.astype(BF16), gw_ref[...], preferred_element_type=F32) + gb_ref[...]
    y5 = y5 * _sigmoid(gl) * _silu(g5_ref[...])
    parts = [ym_ref[...], ys_ref[...], yw_ref[...], y5.astype(BF16)]
    acc = None
    for i, part in enumerate(parts):
        d = jnp.dot(part, w_ref[i * PROJ_TILE:(i + 1) * PROJ_TILE, :], preferred_element_type=F32)
        acc = d if acc is None else acc + d
    ms = jnp.mean(acc * acc, axis=-1, keepdims=True)
    o_ref[...] = x_ref[...] + acc * lax.rsqrt(ms + NORM_EPS) * pg_ref[...]


def _outproj(y_moba, y_ssd, y_swa, y_s5, proj, x2, w_out, post_g, s5_d, glu_w, glu_b, layer, tm):
    m = x2.shape[0]
    rows = lambda w: pl.BlockSpec((tm, w), lambda i: (i, 0))
    full = lambda shape: pl.BlockSpec(shape, lambda i: (0, 0))
    of_layer = lambda shape: pl.BlockSpec((None,) + shape, lambda i: (layer, 0, 0))
    return pl.pallas_call(
        _outproj_kernel,
        grid=(m // tm,),
        in_specs=[
            rows(MOBA_W), rows(SSD_W), rows(SWA_W), rows(S5_W),
            pl.BlockSpec((tm, PROJ_TILE), lambda i: (i, T_SU)),
            pl.BlockSpec((tm, PROJ_TILE), lambda i: (i, T_S5G)),
            rows(D_MODEL),
            of_layer((MIX_W, D_MODEL)),
            full((1, D_MODEL)),
            full((1, S5_W)),
            of_layer((S5_W, S5_W)),
            full((1, S5_W)),
        ],
        out_specs=rows(D_MODEL),
        out_shape=jax.ShapeDtypeStruct((m, D_MODEL), F32),
        compiler_params=pltpu.CompilerParams(
            dimension_semantics=("parallel",),
            vmem_limit_bytes=VMEM_LIMIT_BYTES),
        name="outproj",
    )(y_moba, y_ssd, y_swa, y_s5, proj, proj, x2, w_out, post_g, s5_d, glu_w, glu_b)


def _swa_tiles(w, axis):
    s = w.shape
    per_kv = SWA_HEADS // SWA_KV_HEADS
    w = w.reshape(s[:axis] + (SWA_KV_HEADS, per_kv, HEAD_DIM) + s[axis + 1:])
    return jnp.swapaxes(w, axis, axis + 1).reshape(s)


def _rotary_pairs(w):
    s = w.shape
    w = w.reshape(s[:-1] + (s[-1] // LANES, 2, 2, HEAD_DIM // 2))
    return jnp.swapaxes(w, -2, -3).reshape(s)


def _rope_tables(seq):
    inv = 1.0 / (ROPE_THETA ** (jnp.arange(0, HEAD_DIM, 2, dtype=F32) / HEAD_DIM))
    ang = jnp.arange(seq, dtype=F32)[:, None] * inv[None, :]
    cos, sin = jnp.cos(ang), jnp.sin(ang)
    cos_t = jnp.concatenate([cos, cos, cos, cos], axis=1)
    sin_t = jnp.concatenate([-sin, -sin, sin, sin], axis=1)
    reps = PROJ_TILE // LANES
    return jnp.tile(cos_t, (1, reps)), jnp.tile(sin_t, (1, reps))


def _cat_w_in(w_in):
    offs = np.cumsum([0, MOBA_W, MOBA_W, MOBA_W, MOBA_W, SSD_CONV_CH, SSD_HEADS, SSD_W,
                      SWA_W, SWA_KV_W, SWA_KV_W, SWA_W, S5_W, S5_W])
    (o_mq, o_mk, o_mv, o_mg, o_xbc, o_dt, o_z, o_sq, o_sk, o_sv, o_sg, o_su, o_s5g, _) = [int(v) for v in offs]
    sl = lambda o, w: w_in[..., o:o + w]
    pad = jnp.zeros(w_in.shape[:-1] + (PROJ_TILE - 2 * SWA_KV_W - SSD_HEADS,), w_in.dtype)
    pieces = [_rotary_pairs(sl(o_mq, MOBA_W)), _rotary_pairs(sl(o_mk, MOBA_W)),
              sl(o_mv, MOBA_W), sl(o_mg, MOBA_W),
              sl(o_xbc, SSD_CONV_CH), sl(o_z, SSD_W),
              _rotary_pairs(_swa_tiles(sl(o_sq, SWA_W), 2)), _swa_tiles(sl(o_sg, SWA_W), 2),
              sl(o_su, S5_W), sl(o_s5g, S5_W),
              _rotary_pairs(sl(o_sk, SWA_KV_W)), sl(o_sv, SWA_KV_W), sl(o_dt, SSD_HEADS), pad]
    return jnp.concatenate(pieces, axis=-1).astype(BF16)


def _pad_lanes(v):
    return jnp.pad(v, (0, LANES - v.shape[0]))[None, :]


def _layer(layer, x2, b, seq, cos_t, sin_t, pre_g, post_g, w_in, w_out, conv_w, conv_b, dt_bias, a_log,
           ssd_d, ssd_norm, sinks, s5_tabs, s5_d, glu_w, glu_b):
    proj = _inproj(x2, pre_g[None, :], w_in, layer, cos_t, sin_t, seq, min(TM_INPROJ, seq))
    proj3 = proj.reshape(b, seq, PROJ_W)

    y_moba = _moba(proj3)
    y_swa = _swa(proj3, sinks)
    y_ssd = _ssd(proj3, conv_w, conv_b[None, :], _pad_lanes(dt_bias), _pad_lanes(a_log),
                 jnp.repeat(ssd_d, SSD_HEAD_DIM)[None, :], ssd_norm[None, :])
    y_s5 = _s5_core(proj3, *s5_tabs, layer).reshape(b * seq, S5_W)

    m = b * seq
    return _outproj(y_moba.reshape(m, MOBA_W), y_ssd.reshape(m, SSD_W), y_swa.reshape(m, SWA_W),
                    y_s5, proj, x2, w_out, post_g[None, :], s5_d[None, :],
                    glu_w, glu_b[None, :], layer, min(TM_OUTPROJ, seq))


def kernel(x, pre_norm, post_norm, w_in, w_out, ssd_conv_w, ssd_conv_b, ssd_dt_bias, ssd_a_log, ssd_d, ssd_norm, swa_sinks, s5_a_re, s5_a_im, s5_log_dt, s5_b_re, s5_b_im, s5_c_re, s5_c_im, s5_d, s5_glu_w, s5_glu_b):
    b, seq, d = x.shape
    assert d == D_MODEL
    assert seq % MOBA_BLOCK == 0 and seq % SSD_CHUNK == 0 and seq % SWA_WINDOW == 0
    assert seq % (S5_CHUNK * S5_SCAN_ROWS) == 0
    assert seq % SWA_TQ == 0
    depth = pre_norm.shape[0]
    cos_t, sin_t = _rope_tables(seq)
    w_cat = _cat_w_in(w_in)
    swa_lo = MOBA_W + SSD_W
    w_out_p = jnp.concatenate([w_out[:, :swa_lo], _swa_tiles(w_out[:, swa_lo:swa_lo + SWA_W], 1),
                               w_out[:, swa_lo + SWA_W:]], axis=1).astype(BF16)
    per_kv = SWA_HEADS // SWA_KV_HEADS
    sinks_p = jnp.swapaxes(swa_sinks.reshape(depth, SWA_KV_HEADS, per_kv), 1, 2).reshape(depth, SWA_HEADS)
    glu_w16 = s5_glu_w.astype(BF16)
    fold = lambda a: a.reshape((depth * S5_GROUPS,) + a.shape[2:])
    tabs = _s5_tables(fold(s5_a_re), fold(s5_a_im), s5_log_dt.reshape(-1), fold(s5_b_re), fold(s5_b_im),
                      fold(s5_c_re), fold(s5_c_im))

    x2 = x.reshape(b * seq, d)
    for l in range(depth):
        x2 = _layer(l, x2, b, seq, cos_t, sin_t, pre_norm[l], post_norm[l], w_cat, w_out_p,
                    ssd_conv_w[l], ssd_conv_b[l], ssd_dt_bias[l], ssd_a_log[l], ssd_d[l], ssd_norm[l],
                    sinks_p[l], tabs, s5_d[l], glu_w16, s5_glu_b[l])
    return x2.reshape(b, seq, d)
```

```python
import functools
import math

import jax
import jax.numpy as jnp
import numpy as np
from jax import lax
from jax.experimental import pallas as pl
from jax.experimental.pallas import tpu as pltpu

F32 = jnp.float32
BF16 = jnp.bfloat16

D_MODEL = 2048
HEAD_DIM = 64
ROPE_THETA = 10000.0
NORM_EPS = 1e-6
MOBA_HEADS = 8
MOBA_W = MOBA_HEADS * HEAD_DIM
MOBA_BLOCK = 256
MOBA_TOPK = 3
SSD_HEADS = 8
SSD_HEAD_DIM = 64
SSD_W = SSD_HEADS * SSD_HEAD_DIM
SSD_GROUPS = 2
SSD_STATE = 128
SSD_CONV = 4
SSD_CHUNK = 256
SSD_CONV_CH = SSD_W + 2 * SSD_GROUPS * SSD_STATE
SWA_HEADS = 8
SWA_KV_HEADS = 2
SWA_W = SWA_HEADS * HEAD_DIM
SWA_KV_W = SWA_KV_HEADS * HEAD_DIM
SWA_WINDOW = 128
S5_W = 512
S5_GROUP = 16
S5_GROUPS = S5_W // S5_GROUP
S5_STATE = 64
MIX_W = MOBA_W + SSD_W + SWA_W + S5_W

LANES = 128
SUBLANES = 8
VMEM_LIMIT_BYTES = 48 * 1024 * 1024

PROJ_TILE = 512
T_MQ, T_MK, T_MV, T_MG, T_XBC, T_Z, T_SQ, T_SG, T_SU, T_S5G, T_TAIL = 0, 1, 2, 3, 4, 6, 7, 8, 9, 10, 11
PROJ_W = 12 * PROJ_TILE
TAIL_K_BLK = T_TAIL * (PROJ_TILE // LANES)
TAIL_V_BLK = TAIL_K_BLK + 1
TAIL_DT_BLK = TAIL_K_BLK + 2

NEG = -1e30
TM_INPROJ = 1024
TM_OUTPROJ = 512
SWA_TQ = 512
MOBA_PAIRS = 4
MOBA_ONES_ROWS = 16
S5_CHUNK = 8
S5_TILE_GROUPS = LANES // S5_GROUP
S5_SCAN_ROWS = SUBLANES


def _silu(x):
    return x / (1.0 + jnp.exp(-x))


def _sigmoid(x):
    return 1.0 / (1.0 + jnp.exp(-x))


def _bf16_parts(x):
    hi = x.astype(BF16)
    r1 = x - hi.astype(F32)
    mid = r1.astype(BF16)
    lo = (r1 - mid.astype(F32)).astype(BF16)
    return hi, mid, lo


def _nt_dot(a, b, precision=None):
    return lax.dot_general(a, b, (((1,), (1,)), ((), ())), precision=precision,
                           preferred_element_type=F32)


def _tn_dot(a, b, precision=None):
    return lax.dot_general(a, b, (((0,), (0,)), ((), ())), precision=precision,
                           preferred_element_type=F32)


def _inproj_kernel(x_ref, g_ref, w_ref, cos_ref, sin_ref, o_ref, h_scr):
    j = pl.program_id(1)

    @pl.when(j == 0)
    def _():
        x = x_ref[...]
        ms = jnp.mean(x * x, axis=-1, keepdims=True)
        h_scr[...] = (x * lax.rsqrt(ms + NORM_EPS) * g_ref[...]).astype(BF16)

    acc = jnp.dot(h_scr[...], w_ref[...], preferred_element_type=F32)
    is_rope = (j == T_MQ) | (j == T_MK) | (j == T_SQ) | (j == T_TAIL)

    @pl.when(is_rope)
    def _():
        cos, sin = cos_ref[...], sin_ref[...]
        roped = jnp.concatenate(
            [acc[:, t * LANES:(t + 1) * LANES] * cos
             + pltpu.roll(acc[:, t * LANES:(t + 1) * LANES], LANES // 2, axis=1) * sin
             for t in range(acc.shape[1] // LANES)], axis=1)
        lane = lax.broadcasted_iota(jnp.int32, acc.shape, 1)
        apply = (j != T_TAIL) | (lane < SWA_KV_W)
        o_ref[...] = jnp.where(apply, roped, acc)

    @pl.when(jnp.logical_not(is_rope))
    def _():
        o_ref[...] = acc


def _inproj(x2, pre_g, w_cat, layer, cos_t, sin_t, seq, tm):
    m = x2.shape[0]
    nseq = seq // tm
    return pl.pallas_call(
        _inproj_kernel,
        grid=(m // tm, PROJ_W // PROJ_TILE),
        in_specs=[
            pl.BlockSpec((tm, D_MODEL), lambda i, j: (i, 0)),
            pl.BlockSpec((1, D_MODEL), lambda i, j: (0, 0)),
            pl.BlockSpec((None, D_MODEL, PROJ_TILE), lambda i, j: (layer, 0, j)),
            pl.BlockSpec((tm, LANES), lambda i, j: (i % nseq, 0)),
            pl.BlockSpec((tm, LANES), lambda i, j: (i % nseq, 0)),
        ],
        out_specs=pl.BlockSpec((tm, PROJ_TILE), lambda i, j: (i, j)),
        out_shape=jax.ShapeDtypeStruct((m, PROJ_W), F32),
        scratch_shapes=[pltpu.VMEM((tm, D_MODEL), BF16)],
        compiler_params=pltpu.CompilerParams(
            dimension_semantics=("parallel", "arbitrary"),
            vmem_limit_bytes=VMEM_LIMIT_BYTES),
        name="inproj",
    )(x2, pre_g, w_cat, cos_t, sin_t)


def _pair_head(idx):
    return (idx // (HEAD_DIM // 2)) % 2


def _moba_kernel(q_ref, k_ref, v_ref, g_ref, o_ref, kaug_scr, vt_scr, kmean_scr, qaug_scr,
                 acc_scr, sa_scr, sb_scr, *, nb):
    qi = pl.program_id(2)
    blk = MOBA_BLOCK
    qscale = (HEAD_DIM ** -0.5) * math.log2(math.e)

    nh = 2 * MOBA_PAIRS

    @pl.when(qi == 0)
    def _():
        seq = k_ref.shape[0]
        row_blk = lax.broadcasted_iota(jnp.int32, (seq, LANES), 0) // blk
        lane = lax.broadcasted_iota(jnp.int32, (seq, LANES), 1)
        onehot = jnp.where(row_blk == lane, 1.0, 0.0).astype(BF16)
        kmean_scr[...] = jnp.mean(k_ref[...].reshape(nb, blk, MOBA_PAIRS * LANES), axis=1)
        for pp in range(MOBA_PAIRS):
            kaug_scr[pp, :, :LANES] = k_ref[:, pp * LANES:(pp + 1) * LANES].astype(BF16)
            kaug_scr[pp, :, LANES:] = onehot
        ones = jnp.ones((MOBA_ONES_ROWS, blk), BF16)
        for n in range(nb):
            vt = v_ref[n * blk:(n + 1) * blk, :].T.astype(BF16)
            for hd in range(nh):
                vt_scr[hd, n, 0:HEAD_DIM, :] = vt[hd * HEAD_DIM:(hd + 1) * HEAD_DIM, :]
                vt_scr[hd, n, HEAD_DIM:, :] = ones

    q_all = q_ref[...].T
    sub = lax.broadcasted_iota(jnp.int32, (LANES, blk), 0)
    klane = lax.broadcasted_iota(jnp.int32, (nb, LANES), 1)
    blk_id = lax.broadcasted_iota(jnp.int32, (nb, blk), 0)
    past = blk_id < qi

    for hd in range(nh):
        pp, hh = divmod(hd, 2)
        q_t = q_all[pp * LANES:(pp + 1) * LANES]
        km = jnp.where(_pair_head(klane) == hh, kmean_scr[:, pp * LANES:(pp + 1) * LANES], 0.0)
        gate = jnp.dot(km, q_t, precision=lax.Precision.HIGHEST, preferred_element_type=F32)
        gm = jnp.where(past, gate, -jnp.inf)
        cnt = jnp.zeros((nb, blk), F32)
        for n in range(nb):
            gn = gm[n:n + 1, :]
            beats = (gn > gm) | ((gn == gm) & (blk_id > n))
            cnt = cnt + jnp.where(beats & (qi > n), 1.0, 0.0)
        keep = (past & (cnt < float(MOBA_TOPK))) | (blk_id == qi)
        pen = jnp.where(keep, 0.0, NEG)
        pen = jnp.concatenate([pen, jnp.zeros((LANES - nb, blk), F32)], axis=0)
        qh = jnp.where(_pair_head(sub) == hh, q_t, 0.0) * qscale
        qaug_scr[hd] = jnp.concatenate([qh.astype(BF16), pen.astype(BF16)], axis=0)
        acc_scr[hd] = jnp.zeros(acc_scr.shape[1:], F32)

    def qk(t, s_ref):
        start = pl.multiple_of(t * blk, blk)
        for pp in range(MOBA_PAIRS):
            kb = kaug_scr[pp, pl.ds(start, blk), :]
            for hd in (2 * pp, 2 * pp + 1):
                s_ref[hd] = jnp.dot(kb, qaug_scr[hd], preferred_element_type=F32)

    def process(t, s_ref, ms, tail):
        out = []
        for hd in range(nh):
            s = s_ref[hd]
            if tail:
                s = jnp.where(key_minus_qry <= (qi - t) * blk, s, NEG)
            m_new = jnp.maximum(ms[hd], jnp.max(s, axis=0, keepdims=True))
            alpha = jnp.exp2(ms[hd] - m_new)
            p = jnp.exp2(s - m_new).astype(BF16)
            pv = jnp.dot(vt_scr[hd, jnp.minimum(t, qi)], p, preferred_element_type=F32)
            acc_scr[hd] = alpha * acc_scr[hd] + pv
            out.append(m_new)
        return tuple(out)

    qk(0, sa_scr)

    def body(k, ms):
        t = 2 * k
        qk(t + 1, sb_scr)
        ms = process(t, sa_scr, ms, False)
        qk(t + 2, sa_scr)
        return process(t + 1, sb_scr, ms, False)

    m_init = jnp.full((1, blk), NEG, F32)
    ms = lax.fori_loop(0, qi // 2, body, (m_init,) * nh)
    key_minus_qry = (lax.broadcasted_iota(jnp.int32, (blk, blk), 0)
                     - lax.broadcasted_iota(jnp.int32, (blk, blk), 1))
    t1 = 2 * (qi // 2)
    has_second = t1 < qi

    @pl.when(has_second)
    def _():
        qk(t1 + 1, sb_scr)
        process(t1 + 1, sb_scr, process(t1, sa_scr, ms, False), True)

    @pl.when(jnp.logical_not(has_second))
    def _():
        process(t1, sa_scr, ms, True)

    outs = [acc_scr[hd][:HEAD_DIM] / acc_scr[hd][HEAD_DIM:HEAD_DIM + 1] for hd in range(nh)]
    out = jnp.concatenate(outs, axis=0).T
    o_ref[...] = (out * _silu(g_ref[...])).astype(o_ref.dtype)


def _moba(proj3):
    b, seq, _ = proj3.shape
    blk = MOBA_BLOCK
    nb = seq // blk
    wid = MOBA_PAIRS * LANES
    ngrp = MOBA_W // wid
    per = PROJ_TILE // wid
    nh = 2 * MOBA_PAIRS
    return pl.pallas_call(
        functools.partial(_moba_kernel, nb=nb),
        grid=(b, ngrp, nb),
        in_specs=[
            pl.BlockSpec((None, blk, wid), lambda bi, hp, qi: (bi, qi, T_MQ * per + hp)),
            pl.BlockSpec((None, seq, wid), lambda bi, hp, qi: (bi, 0, T_MK * per + hp),
                         pipeline_mode=pl.Buffered(1)),
            pl.BlockSpec((None, seq, wid), lambda bi, hp, qi: (bi, 0, T_MV * per + hp),
                         pipeline_mode=pl.Buffered(1)),
            pl.BlockSpec((None, blk, wid), lambda bi, hp, qi: (bi, qi, T_MG * per + hp)),
        ],
        out_specs=pl.BlockSpec((None, blk, wid), lambda bi, hp, qi: (bi, qi, hp)),
        out_shape=jax.ShapeDtypeStruct((b, seq, MOBA_W), BF16),
        scratch_shapes=[pltpu.VMEM((MOBA_PAIRS, seq, 2 * LANES), BF16),
                        pltpu.VMEM((nh, nb, HEAD_DIM + MOBA_ONES_ROWS, blk), BF16),
                        pltpu.VMEM((nb, wid), F32),
                        pltpu.VMEM((nh, 2 * LANES, blk), BF16),
                        pltpu.VMEM((nh, HEAD_DIM + MOBA_ONES_ROWS, blk), F32),
                        pltpu.VMEM((nh, blk, blk), F32),
                        pltpu.VMEM((nh, blk, blk), F32)],
        compiler_params=pltpu.CompilerParams(
            dimension_semantics=("parallel", "parallel", "arbitrary"),
            vmem_limit_bytes=VMEM_LIMIT_BYTES),
        name="moba",
    )(proj3, proj3, proj3, proj3)


def _swa_kernel(sink_ref, q_ref, kp_ref, kc_ref, vp_ref, vc_ref, g_ref, o_ref):
    n = pl.program_id(1)
    w = SWA_WINDOW
    scale = HEAD_DIM ** -0.5
    kcat = jnp.concatenate([kp_ref[...], kc_ref[...]], axis=0).astype(BF16)
    vcat = jnp.concatenate([vp_ref[...], vc_ref[...]], axis=0).astype(BF16)
    row = lax.broadcasted_iota(jnp.int32, (w, 2 * w), 0)
    col = lax.broadcasted_iota(jnp.int32, (w, 2 * w), 1)
    band = (col > row) & (col <= row + w)
    lane = lax.broadcasted_iota(jnp.int32, (w, LANES), 1)
    for win in range(SWA_TQ // w):
        valid = band & ((n > 0) | (col >= w)) if win == 0 else band
        kw = kcat[win * w:(win + 2) * w]
        vw = vcat[win * w:(win + 2) * w]
        rows = slice(win * w, (win + 1) * w)
        for i in range(SWA_W // LANES):
            q_t = q_ref[rows, i * LANES:(i + 1) * LANES]
            halves = []
            for c in range(2):
                qm = (jnp.where(_pair_head(lane) == c, q_t, 0.0) * scale).astype(BF16)
                s = _nt_dot(qm, kw)
                s = jnp.where(valid, s, NEG)
                sink = sink_ref[2 * i + c]
                m = jnp.maximum(jnp.max(s, axis=-1, keepdims=True), sink)
                e = jnp.exp(s - m)
                denom = jnp.sum(e, axis=-1, keepdims=True) + jnp.exp(sink - m)
                o = jnp.dot(e.astype(BF16), vw, preferred_element_type=F32)
                halves.append(o / denom)
            out = jnp.where(lane < HEAD_DIM, halves[0], halves[1])
            g_t = g_ref[rows, i * LANES:(i + 1) * LANES]
            o_ref[rows, i * LANES:(i + 1) * LANES] = (out * _silu(g_t)).astype(o_ref.dtype)


def _swa(proj3, sinks_perm):
    b, seq, _ = proj3.shape
    w = SWA_WINDOW
    tq = SWA_TQ
    per = tq // w
    prev = lambda bi, n: (bi, jnp.maximum(n * per - 1, 0), TAIL_K_BLK)
    prev_v = lambda bi, n: (bi, jnp.maximum(n * per - 1, 0), TAIL_V_BLK)
    return pl.pallas_call(
        _swa_kernel,
        grid=(b, seq // tq),
        in_specs=[
            pl.BlockSpec(memory_space=pltpu.SMEM),
            pl.BlockSpec((None, tq, SWA_W), lambda bi, n: (bi, n, T_SQ)),
            pl.BlockSpec((None, w, LANES), prev),
            pl.BlockSpec((None, tq, LANES), lambda bi, n: (bi, n, TAIL_K_BLK)),
            pl.BlockSpec((None, w, LANES), prev_v),
            pl.BlockSpec((None, tq, LANES), lambda bi, n: (bi, n, TAIL_V_BLK)),
            pl.BlockSpec((None, tq, SWA_W), lambda bi, n: (bi, n, T_SG)),
        ],
        out_specs=pl.BlockSpec((None, tq, SWA_W), lambda bi, n: (bi, n, 0)),
        out_shape=jax.ShapeDtypeStruct((b, seq, SWA_W), BF16),
        compiler_params=pltpu.CompilerParams(
            dimension_semantics=("parallel", "arbitrary"),
            vmem_limit_bytes=VMEM_LIMIT_BYTES),
        name="swa",
    )(sinks_perm, proj3, proj3, proj3, proj3, proj3, proj3)


def _ssd_kernel(xbc_ref, z_ref, dt_ref, cw_ref, cb_ref, dtb_ref, alog_ref, dsk_ref, nw_ref,
                o_ref, xbuf, state):
    c = pl.program_id(1)
    s = SSD_CHUNK
    pad = SUBLANES
    hi = lax.Precision.HIGHEST

    @pl.when(c == 0)
    def _():
        xbuf[0:pad, :] = jnp.zeros((pad, SSD_CONV_CH), F32)
        state[...] = jnp.zeros_like(state)

    @pl.when(c > 0)
    def _():
        xbuf[0:pad, :] = xbuf[s:s + pad, :]

    xbuf[pad:pad + s, :] = xbc_ref[...]
    conv = cb_ref[...]
    for k in range(SSD_CONV):
        off = pad - (SSD_CONV - 1) + k
        conv = conv + cw_ref[k:k + 1, :] * xbuf[off:off + s, :]
    xbc = _silu(conv)
    xs = xbc[:, :SSD_W]
    gw = SSD_GROUPS * SSD_STATE
    bm = xbc[:, SSD_W:SSD_W + gw]
    cm = xbc[:, SSD_W + gw:]

    dtr = dt_ref[...] + dtb_ref[...]
    dt = jnp.maximum(dtr, 0.0) + jnp.log(1.0 + jnp.exp(-jnp.abs(dtr)))
    a = dt * (-jnp.exp(alog_ref[...]))
    row = lax.broadcasted_iota(jnp.int32, (s, s), 0)
    col = lax.broadcasted_iota(jnp.int32, (s, s), 1)
    lower = row >= col
    tri = jnp.where(lower, 1.0, 0.0).astype(BF16)
    acum = sum(jnp.dot(tri, part, preferred_element_type=F32) for part in _bf16_parts(a))
    acum_t = acum.T
    alast = acum[s - 1:s, :]

    erow = lax.broadcasted_iota(jnp.int32, (LANES, SSD_W), 0)
    ecol = lax.broadcasted_iota(jnp.int32, (LANES, SSD_W), 1)
    expand = jnp.where(erow == ecol // SSD_HEAD_DIM, 1.0, 0.0).astype(BF16)
    cols = jnp.concatenate([dt, jnp.exp(alast - acum), jnp.exp(acum),
                            jnp.broadcast_to(jnp.exp(alast), (SUBLANES, LANES))], axis=0)
    cols_e = sum(jnp.dot(part, expand, preferred_element_type=F32) for part in _bf16_parts(cols))
    dt_e, dec_e, ea_e = cols_e[:s], cols_e[s:2 * s], cols_e[2 * s:3 * s]
    cd_e = cols_e[3 * s:3 * s + 1]

    x_dt = xs * dt_e
    x_dec = (x_dt * dec_e).astype(BF16)
    x_dt16 = x_dt.astype(BF16)
    lane = lax.broadcasted_iota(jnp.int32, (s, LANES), 1)
    hpg = SSD_HEADS // SSD_GROUPS
    gwid = hpg * SSD_HEAD_DIM
    y_parts = []
    for g in range(SSD_GROUPS):
        bg = bm[:, g * SSD_STATE:(g + 1) * SSD_STATE].astype(BF16)
        cg = cm[:, g * SSD_STATE:(g + 1) * SSD_STATE].astype(BF16)
        cbm = _nt_dot(cg, bg)
        st = state[:, g * gwid:(g + 1) * gwid]
        y_off = jnp.dot(cg, st.astype(BF16), preferred_element_type=F32) \
            * ea_e[:, g * gwid:(g + 1) * gwid]
        new_st = cd_e[:, g * gwid:(g + 1) * gwid] * st + _tn_dot(bg, x_dec[:, g * gwid:(g + 1) * gwid])
        state[:, g * gwid:(g + 1) * gwid] = new_st
        for i in range(hpg // 2):
            lo = g * gwid + i * LANES
            xp = x_dt16[:, lo:lo + LANES]
            ys = []
            for hh in range(2):
                h = g * hpg + 2 * i + hh
                seg = acum[:, h:h + 1] - acum_t[h:h + 1, :]
                lm = jnp.where(lower, jnp.exp(jnp.minimum(seg, 0.0)), 0.0)
                ys.append(jnp.dot((cbm * lm).astype(BF16), xp, preferred_element_type=F32))
            y_parts.append(jnp.where(lane < SSD_HEAD_DIM, ys[0], ys[1])
                           + y_off[:, i * LANES:(i + 1) * LANES])
    y = jnp.concatenate(y_parts, axis=1)
    y = (y + dsk_ref[...] * xs) * _silu(z_ref[...])
    outs = []
    for g in range(SSD_GROUPS):
        yg = y[:, g * gwid:(g + 1) * gwid]
        outs.append(yg * lax.rsqrt(jnp.mean(yg * yg, axis=-1, keepdims=True) + NORM_EPS))
    o_ref[...] = (jnp.concatenate(outs, axis=1) * nw_ref[...]).astype(o_ref.dtype)


def _ssd(proj3, conv_w, conv_b, dt_bias_p, a_log_p, d_skip_e, norm_w):
    b, seq, _ = proj3.shape
    s = SSD_CHUNK
    full = lambda shape: pl.BlockSpec(shape, lambda bi, c: (0, 0))
    return pl.pallas_call(
        _ssd_kernel,
        grid=(b, seq // s),
        in_specs=[
            pl.BlockSpec((None, s, SSD_CONV_CH), lambda bi, c: (bi, c, T_XBC * PROJ_TILE // SSD_CONV_CH)),
            pl.BlockSpec((None, s, SSD_W), lambda bi, c: (bi, c, T_Z)),
            pl.BlockSpec((None, s, LANES), lambda bi, c: (bi, c, TAIL_DT_BLK)),
            full((SSD_CONV, SSD_CONV_CH)),
            full((1, SSD_CONV_CH)),
            full((1, LANES)),
            full((1, LANES)),
            full((1, SSD_W)),
            full((1, SSD_W)),
        ],
        out_specs=pl.BlockSpec((None, s, SSD_W), lambda bi, c: (bi, c, 0)),
        out_shape=jax.ShapeDtypeStruct((b, seq, SSD_W), BF16),
        scratch_shapes=[pltpu.VMEM((s + 2 * SUBLANES, SSD_CONV_CH), F32),
                        pltpu.VMEM((SSD_STATE, SSD_W), F32)],
        compiler_params=pltpu.CompilerParams(
            dimension_semantics=("parallel", "arbitrary"),
            vmem_limit_bytes=VMEM_LIMIT_BYTES),
        name="ssd",
    )(proj3, proj3, proj3, conv_w, conv_b, dt_bias_p, a_log_p, d_skip_e, norm_w)


def _s5_kernel(u_ref, w_ref, k_ref, v_ref, are_ref, aim_ref, y_ref, x_scr, e_scr, s_scr, y_scr,
               *, nrow):
    t = S5_CHUNK
    half = S5_TILE_GROUPS * S5_STATE
    for j in range(t):
        x_scr[j // 2, :, (j % 2) * LANES:(j % 2 + 1) * LANES] = (
            u_ref[pl.ds(j, nrow, stride=t), :].astype(BF16))
    e = jnp.dot(x_scr[0], w_ref[0], preferred_element_type=F32)
    for jp in range(1, t // 2):
        e = e + jnp.dot(x_scr[jp], w_ref[jp], preferred_element_type=F32)
    e_scr[...] = e
    a_re = are_ref[...]
    a_im = aim_ref[...]

    def body(it, carry):
        s_re, s_im = carry
        r0 = pl.multiple_of(it * S5_SCAN_ROWS, S5_SCAN_ROWS)
        e8 = e_scr[pl.ds(r0, S5_SCAN_ROWS), :]
        rows = []
        for kk in range(S5_SCAN_ROWS):
            rows.append(jnp.concatenate([s_re, s_im], axis=1))
            e_re, e_im = e8[kk:kk + 1, :half], e8[kk:kk + 1, half:]
            s_re, s_im = a_re * s_re - a_im * s_im + e_re, a_re * s_im + a_im * s_re + e_im
        s_scr[pl.ds(r0, S5_SCAN_ROWS), :] = jnp.concatenate(rows, axis=0)
        return s_re, s_im

    zero = jnp.zeros((1, half), F32)
    lax.fori_loop(0, nrow // S5_SCAN_ROWS, body, (zero, zero))
    y_scr[...] = jnp.dot(s_scr[...].astype(BF16), v_ref[...], preferred_element_type=F32)
    for jp in range(t // 2):
        j = 2 * jp
        y_scr[:, j * LANES:] += jnp.dot(x_scr[jp], k_ref[:, :(t - j) * LANES],
                                        preferred_element_type=F32)
    for i in range(t):
        y_ref[pl.ds(i, nrow, stride=t), :] = y_scr[:, i * LANES:(i + 1) * LANES]


def _s5_core(proj3, w, k, v, a_re, a_im, layer):
    b, seq, _ = proj3.shape
    t = S5_CHUNK
    nrow = seq // t
    ntile = S5_W // LANES
    half = S5_TILE_GROUPS * S5_STATE
    per = PROJ_TILE // LANES
    base = layer * ntile
    return pl.pallas_call(
        functools.partial(_s5_kernel, nrow=nrow),
        grid=(ntile, b),
        in_specs=[
            pl.BlockSpec((None, seq, LANES), lambda q, bi: (bi, 0, T_SU * per + q)),
            pl.BlockSpec((None, t // 2, 2 * LANES, 2 * half), lambda q, bi: (base + q, 0, 0, 0)),
            pl.BlockSpec((None, 2 * LANES, t * LANES), lambda q, bi: (base + q, 0, 0)),
            pl.BlockSpec((None, 2 * half, t * LANES), lambda q, bi: (base + q, 0, 0)),
            pl.BlockSpec((None, 1, half), lambda q, bi: (base + q, 0, 0)),
            pl.BlockSpec((None, 1, half), lambda q, bi: (base + q, 0, 0)),
        ],
        out_specs=pl.BlockSpec((None, seq, LANES), lambda q, bi: (bi, 0, q)),
        out_shape=jax.ShapeDtypeStruct((b, seq, S5_W), F32),
        scratch_shapes=[pltpu.VMEM((t // 2, nrow, 2 * LANES), BF16),
                        pltpu.VMEM((nrow, 2 * half), F32),
                        pltpu.VMEM((nrow, 2 * half), F32),
                        pltpu.VMEM((nrow, t * LANES), F32)],
        compiler_params=pltpu.CompilerParams(
            dimension_semantics=("parallel", "parallel"),
            vmem_limit_bytes=VMEM_LIMIT_BYTES),
        name="s5",
    )(proj3, w, k, v, a_re, a_im)


def _s5_tables(a_re, a_im, log_dt, b_re, b_im, c_re, c_im):
    t = S5_CHUNK
    hi = lax.Precision.HIGHEST
    step = jnp.exp(log_dt)[:, None]
    k = jnp.arange(t + 1, dtype=F32)[None, :, None]
    mag = jnp.exp(k * (a_re * step)[:, None, :])
    ang = k * (a_im * step)[:, None, :]
    pw_re, pw_im = mag * jnp.cos(ang), mag * jnp.sin(ang)
    ab_re, ab_im = pw_re[:, 1], pw_im[:, 1]
    den = a_re * a_re + a_im * a_im
    n_re, n_im = ab_re - 1.0, ab_im
    cf_re = (n_re * a_re + n_im * a_im) / den
    cf_im = (n_im * a_re - n_re * a_im) / den
    bb_re = cf_re[..., None] * b_re - cf_im[..., None] * b_im
    bb_im = cf_re[..., None] * b_im + cf_im[..., None] * b_re
    cp_re = c_re[:, None] * pw_re[:, :t, None, :] - c_im[:, None] * pw_im[:, :t, None, :]
    cp_im = c_re[:, None] * pw_im[:, :t, None, :] + c_im[:, None] * pw_re[:, :t, None, :]
    kern = (jnp.einsum('gshp,gpk->gshk', cp_re, bb_re, precision=hi)
            - jnp.einsum('gshp,gpk->gshk', cp_im, bb_im, precision=hi))
    g = a_re.shape[0]
    tg = S5_TILE_GROUPS
    nt = g // tg
    def tile_diag(m, pattern):
        m5 = jnp.einsum(pattern, m.reshape(nt, tg, t, m.shape[2], m.shape[3]).astype(BF16))
        r, c = m5.shape[3], m5.shape[4]
        rep = (jnp.arange(c)[:, None] == jnp.arange(tg * c)[None, :] % c).astype(BF16)
        out = jnp.einsum('qsrc,cn->qsrn', m5.reshape(nt, t, tg * r, c), rep,
                         preferred_element_type=BF16)
        on_diag = (jnp.arange(tg * r)[:, None] // r) == (jnp.arange(tg * c)[None, :] // c)
        return jnp.where(on_diag, out, jnp.zeros((), BF16))

    k_bd = tile_diag(kern, 'qgsab->qsgba')
    k_rev = (t - 1) - jnp.arange(t, dtype=F32)[None, :, None]
    mag_r = jnp.exp(k_rev * (a_re * step)[:, None, :])
    ang_r = k_rev * (a_im * step)[:, None, :]
    rp_re, rp_im = mag_r * jnp.cos(ang_r), mag_r * jnp.sin(ang_r)
    bt_re, bt_im = bb_re.transpose(0, 2, 1)[:, None], bb_im.transpose(0, 2, 1)[:, None]
    w_re = rp_re[:, :, None, :] * bt_re - rp_im[:, :, None, :] * bt_im
    w_im = rp_re[:, :, None, :] * bt_im + rp_im[:, :, None, :] * bt_re
    w_bd = jnp.concatenate([tile_diag(w_re, 'qgjhp->qjghp'),
                            tile_diag(w_im, 'qgjhp->qjghp')], axis=-1)
    q_re, q_im = pw_re[:, 1:], pw_im[:, 1:]
    v_re = c_re[:, None] * q_re[:, :, None, :] - c_im[:, None] * q_im[:, :, None, :]
    v_im = c_re[:, None] * q_im[:, :, None, :] + c_im[:, None] * q_re[:, :, None, :]
    v_bd = jnp.concatenate([tile_diag(v_re, 'qgihp->qigph'),
                            tile_diag(-v_im, 'qgihp->qigph')], axis=2)
    at_re = pw_re[:, t].reshape(nt, 1, tg * S5_STATE)
    at_im = pw_im[:, t].reshape(nt, 1, tg * S5_STATE)
    w_pair = w_bd.reshape(nt, t // 2, 2 * LANES, w_bd.shape[-1])
    k_all = k_bd.transpose(0, 2, 1, 3).reshape(nt, LANES, t * LANES)
    k_lag = jnp.concatenate([jnp.zeros((nt, LANES, LANES), BF16), k_all[:, :, :-LANES]], axis=2)
    k_pair = jnp.concatenate([k_all, k_lag], axis=1)
    v_all = v_bd.transpose(0, 2, 1, 3).reshape(nt, v_bd.shape[2], t * LANES)
    return w_pair, k_pair, v_all, at_re, at_im


def _outproj_kernel(ym_ref, ys_ref, yw_ref, y5_ref, u_ref, g5_ref, x_ref, w_ref, pg_ref,
                    d5_ref, gw_ref, gb_ref, o_ref):
    y5 = y5_ref[...] + d5_ref[...] * u_ref[...]
    c0 = math.sqrt(2.0 / math.pi)
    y5 = y5 * (0.5 * (1.0 + jnp.tanh(c0 * (y5 + 0.044715 * (y5 * y5 * y5)))))
    gl = jnp.dot(y5.astype(BF16), gw_ref[...], preferred_element_type=F32) + gb_ref[...]
    y5 = y5 * _sigmoid(gl) * _silu(g5_ref[...])
    parts = [ym_ref[...], ys_ref[...], yw_ref[...], y5.astype(BF16)]
    acc = None
    for i, part in enumerate(parts):
        d = jnp.dot(part, w_ref[i * PROJ_TILE:(i + 1) * PROJ_TILE, :], preferred_element_type=F32)
        acc = d if acc is None else acc + d
    ms = jnp.mean(acc * acc, axis=-1, keepdims=True)
    o_ref[...] = x_ref[...] + acc * lax.rsqrt(ms + NORM_EPS) * pg_ref[...]


def _outproj(y_moba, y_ssd, y_swa, y_s5, proj, x2, w_out, post_g, s5_d, glu_w, glu_b, layer, tm):
    m = x2.shape[0]
    rows = lambda w: pl.BlockSpec((tm, w), lambda i: (i, 0))
    full = lambda shape: pl.BlockSpec(shape, lambda i: (0, 0))
    of_layer = lambda shape: pl.BlockSpec((None,) + shape, lambda i: (layer, 0, 0))
    return pl.pallas_call(
        _outproj_kernel,
        grid=(m // tm,),
        in_specs=[
            rows(MOBA_W), rows(SSD_W), rows(SWA_W), rows(S5_W),
            pl.BlockSpec((tm, PROJ_TILE), lambda i: (i, T_SU)),
            pl.BlockSpec((tm, PROJ_TILE), lambda i: (i, T_S5G)),
            rows(D_MODEL),
            of_layer((MIX_W, D_MODEL)),
            full((1, D_MODEL)),
            full((1, S5_W)),
            of_layer((S5_W, S5_W)),
            full((1, S5_W)),
        ],
        out_specs=rows(D_MODEL),
        out_shape=jax.ShapeDtypeStruct((m, D_MODEL), F32),
        compiler_params=pltpu.CompilerParams(
            dimension_semantics=("parallel",),
            vmem_limit_bytes=VMEM_LIMIT_BYTES),
        name="outproj",
    )(y_moba, y_ssd, y_swa, y_s5, proj, proj, x2, w_out, post_g, s5_d, glu_w, glu_b)


def _swa_tiles(w, axis):
    s = w.shape
    per_kv = SWA_HEADS // SWA_KV_HEADS
    w = w.reshape(s[:axis] + (SWA_KV_HEADS, per_kv, HEAD_DIM) + s[axis + 1:])
    return jnp.swapaxes(w, axis, axis + 1).reshape(s)


def _rope_tables(seq):
    inv = 1.0 / (ROPE_THETA ** (jnp.arange(0, HEAD_DIM, 2, dtype=F32) / HEAD_DIM))
    ang = jnp.arange(seq, dtype=F32)[:, None] * inv[None, :]
    cos, sin = jnp.cos(ang), jnp.sin(ang)
    return (jnp.concatenate([cos, cos, cos, cos], axis=1),
            jnp.concatenate([-sin, -sin, sin, sin], axis=1))


(IN_MQ, IN_MK, IN_MV, IN_MG, IN_XBC, IN_DT, IN_Z, IN_SQ, IN_SK, IN_SV, IN_SG, IN_SU, IN_S5G, IN_W) = (
    int(v) for v in np.cumsum([0, MOBA_W, MOBA_W, MOBA_W, MOBA_W, SSD_CONV_CH, SSD_HEADS, SSD_W,
                               SWA_W, SWA_KV_W, SWA_KV_W, SWA_W, S5_W, S5_W]))
WPREP_ROWS = 256


def _wprep_kernel(w_ref, last_ref, o_ref):
    tr = w_ref.shape[0]
    lane = lax.broadcasted_iota(jnp.int32, (tr, LANES), 1)
    half, quarter = LANES // 2, LANES // 4

    def win(o):
        mis = o % LANES
        if mis == 0:
            return w_ref[:, o:o + LANES]
        if o + LANES == IN_W:
            return last_ref[...]
        a = o - mis
        lo, hi = w_ref[:, a:a + LANES], w_ref[:, a + LANES:a + 2 * LANES]
        return jnp.where(lane < LANES - mis, pltpu.roll(lo, LANES - mis, axis=1),
                         pltpu.roll(hi, LANES - mis, axis=1))

    def pairs(v):
        return jnp.where((lane >= quarter) & (lane < half), pltpu.roll(v, LANES - quarter, axis=1),
                         jnp.where((lane >= half) & (lane < half + quarter),
                                   pltpu.roll(v, quarter, axis=1), v))

    def swa_tile(o, i):
        ta, tb = win(o + (i // 2) * LANES), win(o + (i // 2 + 2) * LANES)
        if i % 2 == 0:
            return jnp.where(lane < half, ta, pltpu.roll(tb, half, axis=1))
        return jnp.where(lane < half, pltpu.roll(ta, half, axis=1), tb)

    tiles = []
    per = PROJ_TILE // LANES
    tiles += [pairs(win(IN_MQ + i * LANES)) for i in range(per)]
    tiles += [pairs(win(IN_MK + i * LANES)) for i in range(per)]
    tiles += [win(IN_MV + i * LANES) for i in range(per)]
    tiles += [win(IN_MG + i * LANES) for i in range(per)]
    tiles += [win(IN_XBC + i * LANES) for i in range(SSD_CONV_CH // LANES)]
    tiles += [win(IN_Z + i * LANES) for i in range(per)]
    tiles += [pairs(swa_tile(IN_SQ, i)) for i in range(per)]
    tiles += [swa_tile(IN_SG, i) for i in range(per)]
    tiles += [win(IN_SU + i * LANES) for i in range(per)]
    tiles += [win(IN_S5G + i * LANES) for i in range(per)]
    tiles += [pairs(win(IN_SK)), win(IN_SV),
              jnp.where(lane < SSD_HEADS, win(IN_DT), 0.0), jnp.zeros((tr, LANES), F32)]
    for c, v in enumerate(tiles):
        o_ref[:, c * LANES:(c + 1) * LANES] = v.astype(BF16)


def _cat_w_in(w_in):
    depth, d, _ = w_in.shape
    tr = WPREP_ROWS
    return pl.pallas_call(
        _wprep_kernel,
        grid=(depth, d // tr),
        in_specs=[pl.BlockSpec((None, tr, IN_W), lambda l, r: (l, r, 0)),
                  pl.BlockSpec((None, tr, LANES), lambda l, r: (l, r, 0))],
        out_specs=pl.BlockSpec((None, tr, PROJ_W), lambda l, r: (l, r, 0)),
        out_shape=jax.ShapeDtypeStruct((depth, d, PROJ_W), BF16),
        compiler_params=pltpu.CompilerParams(
            dimension_semantics=("parallel", "parallel"),
            vmem_limit_bytes=VMEM_LIMIT_BYTES),
        name="wprep",
    )(w_in, w_in[:, :, IN_W - LANES:])


def _pad_lanes(v):
    return jnp.pad(v, (0, LANES - v.shape[0]))[None, :]


def _layer(layer, x2, b, seq, cos_t, sin_t, pre_g, post_g, w_in, w_out, conv_w, conv_b, dt_bias, a_log,
           ssd_d, ssd_norm, sinks, s5_tabs, s5_d, glu_w, glu_b):
    proj = _inproj(x2, pre_g[None, :], w_in, layer, cos_t, sin_t, seq, min(TM_INPROJ, seq))
    proj3 = proj.reshape(b, seq, PROJ_W)

    y_moba = _moba(proj3)
    y_swa = _swa(proj3, sinks)
    y_ssd = _ssd(proj3, conv_w, conv_b[None, :], _pad_lanes(dt_bias), _pad_lanes(a_log),
                 jnp.repeat(ssd_d, SSD_HEAD_DIM)[None, :], ssd_norm[None, :])
    y_s5 = _s5_core(proj3, *s5_tabs, layer).reshape(b * seq, S5_W)

    m = b * seq
    return _outproj(y_moba.reshape(m, MOBA_W), y_ssd.reshape(m, SSD_W), y_swa.reshape(m, SWA_W),
                    y_s5, proj, x2, w_out, post_g[None, :], s5_d[None, :],
                    glu_w, glu_b[None, :], layer, min(TM_OUTPROJ, seq))


def kernel(x, pre_norm, post_norm, w_in, w_out, ssd_conv_w, ssd_conv_b, ssd_dt_bias, ssd_a_log, ssd_d, ssd_norm, swa_sinks, s5_a_re, s5_a_im, s5_log_dt, s5_b_re, s5_b_im, s5_c_re, s5_c_im, s5_d, s5_glu_w, s5_glu_b):
    b, seq, d = x.shape
    assert d == D_MODEL
    assert seq % MOBA_BLOCK == 0 and seq % SSD_CHUNK == 0 and seq % SWA_WINDOW == 0
    assert seq % (S5_CHUNK * S5_SCAN_ROWS) == 0
    assert seq % SWA_TQ == 0
    depth = pre_norm.shape[0]
    cos_t, sin_t = _rope_tables(seq)
    w_cat = _cat_w_in(w_in)
    swa_lo = MOBA_W + SSD_W
    w_out_p = jnp.concatenate([w_out[:, :swa_lo], _swa_tiles(w_out[:, swa_lo:swa_lo + SWA_W], 1),
                               w_out[:, swa_lo + SWA_W:]], axis=1).astype(BF16)
    per_kv = SWA_HEADS // SWA_KV_HEADS
    sinks_p = jnp.swapaxes(swa_sinks.reshape(depth, SWA_KV_HEADS, per_kv), 1, 2).reshape(depth, SWA_HEADS)
    glu_w16 = s5_glu_w.astype(BF16)
    fold = lambda a: a.reshape((depth * S5_GROUPS,) + a.shape[2:])
    tabs = _s5_tables(fold(s5_a_re), fold(s5_a_im), s5_log_dt.reshape(-1), fold(s5_b_re), fold(s5_b_im),
                      fold(s5_c_re), fold(s5_c_im))

    x2 = x.reshape(b * seq, d)
    for l in range(depth):
        x2 = _layer(l, x2, b, seq, cos_t, sin_t, pre_norm[l], post_norm[l], w_cat, w_out_p,
                    ssd_conv_w[l], ssd_conv_b[l], ssd_dt_bias[l], ssd_a_log[l], ssd_d[l], ssd_norm[l],
                    sinks_p[l], tabs, s5_d[l], glu_w16, s5_glu_b[l])
    return x2.reshape(b, seq, d)
```

```python
import functools
import math

import jax
import jax.numpy as jnp
import numpy as np
from jax import lax
from jax.experimental import pallas as pl
from jax.experimental.pallas import tpu as pltpu

F32 = jnp.float32
BF16 = jnp.bfloat16

D_MODEL = 2048
HEAD_DIM = 64
ROPE_THETA = 10000.0
NORM_EPS = 1e-6
MOBA_HEADS = 8
MOBA_W = MOBA_HEADS * HEAD_DIM
MOBA_BLOCK = 256
MOBA_TOPK = 3
SSD_HEADS = 8
SSD_HEAD_DIM = 64
SSD_W = SSD_HEADS * SSD_HEAD_DIM
SSD_GROUPS = 2
SSD_STATE = 128
SSD_CONV = 4
SSD_CHUNK = 256
SSD_CONV_CH = SSD_W + 2 * SSD_GROUPS * SSD_STATE
SWA_HEADS = 8
SWA_KV_HEADS = 2
SWA_W = SWA_HEADS * HEAD_DIM
SWA_KV_W = SWA_KV_HEADS * HEAD_DIM
SWA_WINDOW = 128
S5_W = 512
S5_GROUP = 16
S5_GROUPS = S5_W // S5_GROUP
S5_STATE = 64
MIX_W = MOBA_W + SSD_W + SWA_W + S5_W

LANES = 128
SUBLANES = 8
VMEM_LIMIT_BYTES = 48 * 1024 * 1024

PROJ_TILE = 512
T_MQ, T_MK, T_MV, T_MG, T_XBC, T_Z, T_SQ, T_SG, T_SU, T_S5G, T_TAIL = 0, 1, 2, 3, 4, 6, 7, 8, 9, 10, 11
PROJ_W = 12 * PROJ_TILE
TAIL_K_BLK = T_TAIL * (PROJ_TILE // LANES)
TAIL_V_BLK = TAIL_K_BLK + 1
TAIL_DT_BLK = TAIL_K_BLK + 2

NEG = -1e30
TM_INPROJ = 1024
TM_OUTPROJ = 512
SWA_TQ = 512
MOBA_PAIRS = 4
MOBA_ONES_ROWS = 16
S5_CHUNK = 8
S5_TILE_GROUPS = LANES // S5_GROUP
S5_SCAN_ROWS = SUBLANES


def _silu(x):
    return x / (1.0 + jnp.exp(-x))


def _sigmoid(x):
    return 1.0 / (1.0 + jnp.exp(-x))


def _bf16_parts(x):
    hi = x.astype(BF16)
    r1 = x - hi.astype(F32)
    mid = r1.astype(BF16)
    lo = (r1 - mid.astype(F32)).astype(BF16)
    return hi, mid, lo


def _nt_dot(a, b, precision=None):
    return lax.dot_general(a, b, (((1,), (1,)), ((), ())), precision=precision,
                           preferred_element_type=F32)


def _tn_dot(a, b, precision=None):
    return lax.dot_general(a, b, (((0,), (0,)), ((), ())), precision=precision,
                           preferred_element_type=F32)


def _inproj_kernel(x_ref, g_ref, w_ref, cos_ref, sin_ref, o_ref, h_scr):
    j = pl.program_id(1)

    @pl.when(j == 0)
    def _():
        x = x_ref[...]
        ms = jnp.mean(x * x, axis=-1, keepdims=True)
        h_scr[...] = (x * lax.rsqrt(ms + NORM_EPS) * g_ref[...]).astype(BF16)

    acc = _nt_dot(h_scr[...], w_ref[...])
    is_rope = (j == T_MQ) | (j == T_MK) | (j == T_SQ) | (j == T_TAIL)

    @pl.when(is_rope)
    def _():
        cos, sin = cos_ref[...], sin_ref[...]
        roped = jnp.concatenate(
            [acc[:, t * LANES:(t + 1) * LANES] * cos
             + pltpu.roll(acc[:, t * LANES:(t + 1) * LANES], LANES // 2, axis=1) * sin
             for t in range(acc.shape[1] // LANES)], axis=1)
        lane = lax.broadcasted_iota(jnp.int32, acc.shape, 1)
        apply = (j != T_TAIL) | (lane < SWA_KV_W)
        o_ref[...] = jnp.where(apply, roped, acc)

    @pl.when(jnp.logical_not(is_rope))
    def _():
        o_ref[...] = acc


def _inproj(x2, pre_g, w_cat, layer, cos_t, sin_t, seq, tm):
    m = x2.shape[0]
    nseq = seq // tm
    return pl.pallas_call(
        _inproj_kernel,
        grid=(m // tm, PROJ_W // PROJ_TILE),
        in_specs=[
            pl.BlockSpec((tm, D_MODEL), lambda i, j: (i, 0)),
            pl.BlockSpec((1, D_MODEL), lambda i, j: (0, 0)),
            pl.BlockSpec((None, PROJ_TILE, D_MODEL), lambda i, j: (layer, j, 0)),
            pl.BlockSpec((tm, LANES), lambda i, j: (i % nseq, 0)),
            pl.BlockSpec((tm, LANES), lambda i, j: (i % nseq, 0)),
        ],
        out_specs=pl.BlockSpec((tm, PROJ_TILE), lambda i, j: (i, j)),
        out_shape=jax.ShapeDtypeStruct((m, PROJ_W), F32),
        scratch_shapes=[pltpu.VMEM((tm, D_MODEL), BF16)],
        compiler_params=pltpu.CompilerParams(
            dimension_semantics=("parallel", "arbitrary"),
            vmem_limit_bytes=VMEM_LIMIT_BYTES),
        name="inproj",
    )(x2, pre_g, w_cat, cos_t, sin_t)


def _pair_head(idx):
    return (idx // (HEAD_DIM // 2)) % 2


def _moba_kernel(q_ref, k_ref, v_ref, g_ref, o_ref, kaug_scr, vt_scr, kmean_scr, qaug_scr,
                 acc_scr, sa_scr, sb_scr, *, nb):
    qi = pl.program_id(2)
    blk = MOBA_BLOCK
    qscale = (HEAD_DIM ** -0.5) * math.log2(math.e)

    nh = 2 * MOBA_PAIRS

    @pl.when(qi == 0)
    def _():
        seq = k_ref.shape[0]
        row_blk = lax.broadcasted_iota(jnp.int32, (seq, LANES), 0) // blk
        lane = lax.broadcasted_iota(jnp.int32, (seq, LANES), 1)
        onehot = jnp.where(row_blk == lane, 1.0, 0.0).astype(BF16)
        kmean_scr[...] = jnp.mean(k_ref[...].reshape(nb, blk, MOBA_PAIRS * LANES), axis=1)
        for pp in range(MOBA_PAIRS):
            kaug_scr[pp, :, :LANES] = k_ref[:, pp * LANES:(pp + 1) * LANES].astype(BF16)
            kaug_scr[pp, :, LANES:] = onehot
        ones = jnp.ones((MOBA_ONES_ROWS, blk), BF16)
        for n in range(nb):
            vt = v_ref[n * blk:(n + 1) * blk, :].T.astype(BF16)
            for hd in range(nh):
                vt_scr[hd, n, 0:HEAD_DIM, :] = vt[hd * HEAD_DIM:(hd + 1) * HEAD_DIM, :]
                vt_scr[hd, n, HEAD_DIM:, :] = ones

    q_all = q_ref[...].T
    sub = lax.broadcasted_iota(jnp.int32, (LANES, blk), 0)
    klane = lax.broadcasted_iota(jnp.int32, (nb, LANES), 1)
    blk_id = lax.broadcasted_iota(jnp.int32, (nb, blk), 0)
    past = blk_id < qi

    for hd in range(nh):
        pp, hh = divmod(hd, 2)
        q_t = q_all[pp * LANES:(pp + 1) * LANES]
        km = jnp.where(_pair_head(klane) == hh, kmean_scr[:, pp * LANES:(pp + 1) * LANES], 0.0)
        gate = jnp.dot(km, q_t, precision=lax.Precision.HIGHEST, preferred_element_type=F32)
        gm = jnp.where(past, gate, -jnp.inf)
        cnt = jnp.zeros((nb, blk), F32)
        for n in range(nb):
            gn = gm[n:n + 1, :]
            beats = (gn > gm) | ((gn == gm) & (blk_id > n))
            cnt = cnt + jnp.where(beats & (qi > n), 1.0, 0.0)
        keep = (past & (cnt < float(MOBA_TOPK))) | (blk_id == qi)
        pen = jnp.where(keep, 0.0, NEG)
        pen = jnp.concatenate([pen, jnp.zeros((LANES - nb, blk), F32)], axis=0)
        qh = jnp.where(_pair_head(sub) == hh, q_t, 0.0) * qscale
        qaug_scr[hd] = jnp.concatenate([qh.astype(BF16), pen.astype(BF16)], axis=0)
        acc_scr[hd] = jnp.zeros(acc_scr.shape[1:], F32)

    def qk(t, s_ref):
        start = pl.multiple_of(t * blk, blk)
        for pp in range(MOBA_PAIRS):
            kb = kaug_scr[pp, pl.ds(start, blk), :]
            for hd in (2 * pp, 2 * pp + 1):
                s_ref[hd] = jnp.dot(kb, qaug_scr[hd], preferred_element_type=F32)

    def process(t, s_ref, ms, tail):
        out = []
        for hd in range(nh):
            s = s_ref[hd]
            if tail:
                s = jnp.where(key_minus_qry <= (qi - t) * blk, s, NEG)
            m_new = jnp.maximum(ms[hd], jnp.max(s, axis=0, keepdims=True))
            alpha = jnp.exp2(ms[hd] - m_new)
            p = jnp.exp2(s - m_new).astype(BF16)
            pv = jnp.dot(vt_scr[hd, jnp.minimum(t, qi)], p, preferred_element_type=F32)
            acc_scr[hd] = alpha * acc_scr[hd] + pv
            out.append(m_new)
        return tuple(out)

    qk(0, sa_scr)

    def body(k, ms):
        t = 2 * k
        qk(t + 1, sb_scr)
        ms = process(t, sa_scr, ms, False)
        qk(t + 2, sa_scr)
        return process(t + 1, sb_scr, ms, False)

    m_init = jnp.full((1, blk), NEG, F32)
    ms = lax.fori_loop(0, qi // 2, body, (m_init,) * nh)
    key_minus_qry = (lax.broadcasted_iota(jnp.int32, (blk, blk), 0)
                     - lax.broadcasted_iota(jnp.int32, (blk, blk), 1))
    t1 = 2 * (qi // 2)
    has_second = t1 < qi

    @pl.when(has_second)
    def _():
        qk(t1 + 1, sb_scr)
        process(t1 + 1, sb_scr, process(t1, sa_scr, ms, False), True)

    @pl.when(jnp.logical_not(has_second))
    def _():
        process(t1, sa_scr, ms, True)

    outs = [acc_scr[hd][:HEAD_DIM] / acc_scr[hd][HEAD_DIM:HEAD_DIM + 1] for hd in range(nh)]
    out = jnp.concatenate(outs, axis=0).T
    o_ref[...] = (out * _silu(g_ref[...])).astype(o_ref.dtype)


def _moba(proj3):
    b, seq, _ = proj3.shape
    blk = MOBA_BLOCK
    nb = seq // blk
    wid = MOBA_PAIRS * LANES
    ngrp = MOBA_W // wid
    per = PROJ_TILE // wid
    nh = 2 * MOBA_PAIRS
    return pl.pallas_call(
        functools.partial(_moba_kernel, nb=nb),
        grid=(b, ngrp, nb),
        in_specs=[
            pl.BlockSpec((None, blk, wid), lambda bi, hp, qi: (bi, qi, T_MQ * per + hp)),
            pl.BlockSpec((None, seq, wid), lambda bi, hp, qi: (bi, 0, T_MK * per + hp),
                         pipeline_mode=pl.Buffered(1)),
            pl.BlockSpec((None, seq, wid), lambda bi, hp, qi: (bi, 0, T_MV * per + hp),
                         pipeline_mode=pl.Buffered(1)),
            pl.BlockSpec((None, blk, wid), lambda bi, hp, qi: (bi, qi, T_MG * per + hp)),
        ],
        out_specs=pl.BlockSpec((None, blk, wid), lambda bi, hp, qi: (bi, qi, hp)),
        out_shape=jax.ShapeDtypeStruct((b, seq, MOBA_W), BF16),
        scratch_shapes=[pltpu.VMEM((MOBA_PAIRS, seq, 2 * LANES), BF16),
                        pltpu.VMEM((nh, nb, HEAD_DIM + MOBA_ONES_ROWS, blk), BF16),
                        pltpu.VMEM((nb, wid), F32),
                        pltpu.VMEM((nh, 2 * LANES, blk), BF16),
                        pltpu.VMEM((nh, HEAD_DIM + MOBA_ONES_ROWS, blk), F32),
                        pltpu.VMEM((nh, blk, blk), F32),
                        pltpu.VMEM((nh, blk, blk), F32)],
        compiler_params=pltpu.CompilerParams(
            dimension_semantics=("parallel", "parallel", "arbitrary"),
            vmem_limit_bytes=VMEM_LIMIT_BYTES),
        name="moba",
    )(proj3, proj3, proj3, proj3)


def _swa_kernel(sink_ref, q_ref, kp_ref, kc_ref, vp_ref, vc_ref, g_ref, o_ref):
    n = pl.program_id(1)
    w = SWA_WINDOW
    scale = HEAD_DIM ** -0.5
    kcat = jnp.concatenate([kp_ref[...], kc_ref[...]], axis=0).astype(BF16)
    vcat = jnp.concatenate([vp_ref[...], vc_ref[...]], axis=0).astype(BF16)
    row = lax.broadcasted_iota(jnp.int32, (w, 2 * w), 0)
    col = lax.broadcasted_iota(jnp.int32, (w, 2 * w), 1)
    band = (col > row) & (col <= row + w)
    lane = lax.broadcasted_iota(jnp.int32, (w, LANES), 1)
    for win in range(SWA_TQ // w):
        valid = band & ((n > 0) | (col >= w)) if win == 0 else band
        kw = kcat[win * w:(win + 2) * w]
        vw = vcat[win * w:(win + 2) * w]
        rows = slice(win * w, (win + 1) * w)
        for i in range(SWA_W // LANES):
            q_t = q_ref[rows, i * LANES:(i + 1) * LANES]
            halves = []
            for c in range(2):
                qm = (jnp.where(_pair_head(lane) == c, q_t, 0.0) * scale).astype(BF16)
                s = _nt_dot(qm, kw)
                s = jnp.where(valid, s, NEG)
                sink = sink_ref[2 * i + c]
                m = jnp.maximum(jnp.max(s, axis=-1, keepdims=True), sink)
                e = jnp.exp(s - m)
                denom = jnp.sum(e, axis=-1, keepdims=True) + jnp.exp(sink - m)
                o = jnp.dot(e.astype(BF16), vw, preferred_element_type=F32)
                halves.append(o / denom)
            out = jnp.where(lane < HEAD_DIM, halves[0], halves[1])
            g_t = g_ref[rows, i * LANES:(i + 1) * LANES]
            o_ref[rows, i * LANES:(i + 1) * LANES] = (out * _silu(g_t)).astype(o_ref.dtype)


def _swa(proj3, sinks_perm):
    b, seq, _ = proj3.shape
    w = SWA_WINDOW
    tq = SWA_TQ
    per = tq // w
    prev = lambda bi, n: (bi, jnp.maximum(n * per - 1, 0), TAIL_K_BLK)
    prev_v = lambda bi, n: (bi, jnp.maximum(n * per - 1, 0), TAIL_V_BLK)
    return pl.pallas_call(
        _swa_kernel,
        grid=(b, seq // tq),
        in_specs=[
            pl.BlockSpec(memory_space=pltpu.SMEM),
            pl.BlockSpec((None, tq, SWA_W), lambda bi, n: (bi, n, T_SQ)),
            pl.BlockSpec((None, w, LANES), prev),
            pl.BlockSpec((None, tq, LANES), lambda bi, n: (bi, n, TAIL_K_BLK)),
            pl.BlockSpec((None, w, LANES), prev_v),
            pl.BlockSpec((None, tq, LANES), lambda bi, n: (bi, n, TAIL_V_BLK)),
            pl.BlockSpec((None, tq, SWA_W), lambda bi, n: (bi, n, T_SG)),
        ],
        out_specs=pl.BlockSpec((None, tq, SWA_W), lambda bi, n: (bi, n, 0)),
        out_shape=jax.ShapeDtypeStruct((b, seq, SWA_W), BF16),
        compiler_params=pltpu.CompilerParams(
            dimension_semantics=("parallel", "arbitrary"),
            vmem_limit_bytes=VMEM_LIMIT_BYTES),
        name="swa",
    )(sinks_perm, proj3, proj3, proj3, proj3, proj3, proj3)


def _ssd_kernel(xbc_ref, z_ref, dt_ref, cw_ref, cb_ref, dtb_ref, alog_ref, dsk_ref, nw_ref,
                o_ref, xbuf, state):
    c = pl.program_id(1)
    s = SSD_CHUNK
    pad = SUBLANES
    hi = lax.Precision.HIGHEST

    @pl.when(c == 0)
    def _():
        xbuf[0:pad, :] = jnp.zeros((pad, SSD_CONV_CH), F32)
        state[...] = jnp.zeros_like(state)

    @pl.when(c > 0)
    def _():
        xbuf[0:pad, :] = xbuf[s:s + pad, :]

    xbuf[pad:pad + s, :] = xbc_ref[...]
    conv = cb_ref[...]
    for k in range(SSD_CONV):
        off = pad - (SSD_CONV - 1) + k
        conv = conv + cw_ref[k:k + 1, :] * xbuf[off:off + s, :]
    xbc = _silu(conv)
    xs = xbc[:, :SSD_W]
    gw = SSD_GROUPS * SSD_STATE
    bm = xbc[:, SSD_W:SSD_W + gw]
    cm = xbc[:, SSD_W + gw:]

    dtr = dt_ref[...] + dtb_ref[...]
    dt = jnp.maximum(dtr, 0.0) + jnp.log(1.0 + jnp.exp(-jnp.abs(dtr)))
    a = dt * (-jnp.exp(alog_ref[...]))
    row = lax.broadcasted_iota(jnp.int32, (s, s), 0)
    col = lax.broadcasted_iota(jnp.int32, (s, s), 1)
    lower = row >= col
    tri = jnp.where(lower, 1.0, 0.0).astype(BF16)
    acum = sum(jnp.dot(tri, part, preferred_element_type=F32) for part in _bf16_parts(a))
    acum_t = acum.T
    alast = acum[s - 1:s, :]

    erow = lax.broadcasted_iota(jnp.int32, (LANES, SSD_W), 0)
    ecol = lax.broadcasted_iota(jnp.int32, (LANES, SSD_W), 1)
    expand = jnp.where(erow == ecol // SSD_HEAD_DIM, 1.0, 0.0).astype(BF16)
    cols = jnp.concatenate([dt, jnp.exp(alast - acum), jnp.exp(acum),
                            jnp.broadcast_to(jnp.exp(alast), (SUBLANES, LANES))], axis=0)
    cols_e = sum(jnp.dot(part, expand, preferred_element_type=F32) for part in _bf16_parts(cols))
    dt_e, dec_e, ea_e = cols_e[:s], cols_e[s:2 * s], cols_e[2 * s:3 * s]
    cd_e = cols_e[3 * s:3 * s + 1]

    x_dt = xs * dt_e
    x_dec = (x_dt * dec_e).astype(BF16)
    x_dt16 = x_dt.astype(BF16)
    lane = lax.broadcasted_iota(jnp.int32, (s, LANES), 1)
    hpg = SSD_HEADS // SSD_GROUPS
    gwid = hpg * SSD_HEAD_DIM
    y_parts = []
    for g in range(SSD_GROUPS):
        bg = bm[:, g * SSD_STATE:(g + 1) * SSD_STATE].astype(BF16)
        cg = cm[:, g * SSD_STATE:(g + 1) * SSD_STATE].astype(BF16)
        cbm = _nt_dot(cg, bg)
        st = state[:, g * gwid:(g + 1) * gwid]
        y_off = jnp.dot(cg, st.astype(BF16), preferred_element_type=F32) \
            * ea_e[:, g * gwid:(g + 1) * gwid]
        new_st = cd_e[:, g * gwid:(g + 1) * gwid] * st + _tn_dot(bg, x_dec[:, g * gwid:(g + 1) * gwid])
        state[:, g * gwid:(g + 1) * gwid] = new_st
        for i in range(hpg // 2):
            lo = g * gwid + i * LANES
            xp = x_dt16[:, lo:lo + LANES]
            ys = []
            for hh in range(2):
                h = g * hpg + 2 * i + hh
                seg = acum[:, h:h + 1] - acum_t[h:h + 1, :]
                lm = jnp.where(lower, jnp.exp(jnp.minimum(seg, 0.0)), 0.0)
                ys.append(jnp.dot((cbm * lm).astype(BF16), xp, preferred_element_type=F32))
            y_parts.append(jnp.where(lane < SSD_HEAD_DIM, ys[0], ys[1])
                           + y_off[:, i * LANES:(i + 1) * LANES])
    y = jnp.concatenate(y_parts, axis=1)
    y = (y + dsk_ref[...] * xs) * _silu(z_ref[...])
    outs = []
    for g in range(SSD_GROUPS):
        yg = y[:, g * gwid:(g + 1) * gwid]
        outs.append(yg * lax.rsqrt(jnp.mean(yg * yg, axis=-1, keepdims=True) + NORM_EPS))
    o_ref[...] = (jnp.concatenate(outs, axis=1) * nw_ref[...]).astype(o_ref.dtype)


def _ssd(proj3, conv_w, conv_b, dt_bias_p, a_log_p, d_skip_e, norm_w):
    b, seq, _ = proj3.shape
    s = SSD_CHUNK
    full = lambda shape: pl.BlockSpec(shape, lambda bi, c: (0, 0))
    return pl.pallas_call(
        _ssd_kernel,
        grid=(b, seq // s),
        in_specs=[
            pl.BlockSpec((None, s, SSD_CONV_CH), lambda bi, c: (bi, c, T_XBC * PROJ_TILE // SSD_CONV_CH)),
            pl.BlockSpec((None, s, SSD_W), lambda bi, c: (bi, c, T_Z)),
            pl.BlockSpec((None, s, LANES), lambda bi, c: (bi, c, TAIL_DT_BLK)),
            full((SSD_CONV, SSD_CONV_CH)),
            full((1, SSD_CONV_CH)),
            full((1, LANES)),
            full((1, LANES)),
            full((1, SSD_W)),
            full((1, SSD_W)),
        ],
        out_specs=pl.BlockSpec((None, s, SSD_W), lambda bi, c: (bi, c, 0)),
        out_shape=jax.ShapeDtypeStruct((b, seq, SSD_W), BF16),
        scratch_shapes=[pltpu.VMEM((s + 2 * SUBLANES, SSD_CONV_CH), F32),
                        pltpu.VMEM((SSD_STATE, SSD_W), F32)],
        compiler_params=pltpu.CompilerParams(
            dimension_semantics=("parallel", "arbitrary"),
            vmem_limit_bytes=VMEM_LIMIT_BYTES),
        name="ssd",
    )(proj3, proj3, proj3, conv_w, conv_b, dt_bias_p, a_log_p, d_skip_e, norm_w)


def _s5_kernel(u_ref, w_ref, k_ref, v_ref, are_ref, aim_ref, y_ref, x_scr, e_scr, s_scr, y_scr,
               *, nrow):
    t = S5_CHUNK
    half = S5_TILE_GROUPS * S5_STATE
    for j in range(t):
        x_scr[j // 2, :, (j % 2) * LANES:(j % 2 + 1) * LANES] = (
            u_ref[pl.ds(j, nrow, stride=t), :].astype(BF16))
    e = jnp.dot(x_scr[0], w_ref[0], preferred_element_type=F32)
    for jp in range(1, t // 2):
        e = e + jnp.dot(x_scr[jp], w_ref[jp], preferred_element_type=F32)
    e_scr[...] = e
    a_re = are_ref[...]
    a_im = aim_ref[...]

    def body(it, carry):
        s_re, s_im = carry
        r0 = pl.multiple_of(it * S5_SCAN_ROWS, S5_SCAN_ROWS)
        e8 = e_scr[pl.ds(r0, S5_SCAN_ROWS), :]
        rows = []
        for kk in range(S5_SCAN_ROWS):
            rows.append(jnp.concatenate([s_re, s_im], axis=1))
            e_re, e_im = e8[kk:kk + 1, :half], e8[kk:kk + 1, half:]
            s_re, s_im = a_re * s_re - a_im * s_im + e_re, a_re * s_im + a_im * s_re + e_im
        s_scr[pl.ds(r0, S5_SCAN_ROWS), :] = jnp.concatenate(rows, axis=0)
        return s_re, s_im

    zero = jnp.zeros((1, half), F32)
    lax.fori_loop(0, nrow // S5_SCAN_ROWS, body, (zero, zero))
    y_scr[...] = jnp.dot(s_scr[...].astype(BF16), v_ref[...], preferred_element_type=F32)
    for jp in range(t // 2):
        j = 2 * jp
        y_scr[:, j * LANES:] += jnp.dot(x_scr[jp], k_ref[:, :(t - j) * LANES],
                                        preferred_element_type=F32)
    for i in range(t):
        y_ref[pl.ds(i, nrow, stride=t), :] = y_scr[:, i * LANES:(i + 1) * LANES]


def _s5_core(proj3, w, k, v, a_re, a_im, layer):
    b, seq, _ = proj3.shape
    t = S5_CHUNK
    nrow = seq // t
    ntile = S5_W // LANES
    half = S5_TILE_GROUPS * S5_STATE
    per = PROJ_TILE // LANES
    base = layer * ntile
    return pl.pallas_call(
        functools.partial(_s5_kernel, nrow=nrow),
        grid=(ntile, b),
        in_specs=[
            pl.BlockSpec((None, seq, LANES), lambda q, bi: (bi, 0, T_SU * per + q)),
            pl.BlockSpec((None, t // 2, 2 * LANES, 2 * half), lambda q, bi: (base + q, 0, 0, 0)),
            pl.BlockSpec((None, 2 * LANES, t * LANES), lambda q, bi: (base + q, 0, 0)),
            pl.BlockSpec((None, 2 * half, t * LANES), lambda q, bi: (base + q, 0, 0)),
            pl.BlockSpec((None, 1, half), lambda q, bi: (base + q, 0, 0)),
            pl.BlockSpec((None, 1, half), lambda q, bi: (base + q, 0, 0)),
        ],
        out_specs=pl.BlockSpec((None, seq, LANES), lambda q, bi: (bi, 0, q)),
        out_shape=jax.ShapeDtypeStruct((b, seq, S5_W), F32),
        scratch_shapes=[pltpu.VMEM((t // 2, nrow, 2 * LANES), BF16),
                        pltpu.VMEM((nrow, 2 * half), F32),
                        pltpu.VMEM((nrow, 2 * half), F32),
                        pltpu.VMEM((nrow, t * LANES), F32)],
        compiler_params=pltpu.CompilerParams(
            dimension_semantics=("parallel", "parallel"),
            vmem_limit_bytes=VMEM_LIMIT_BYTES),
        name="s5",
    )(proj3, w, k, v, a_re, a_im)


def _s5_tables(a_re, a_im, log_dt, b_re, b_im, c_re, c_im):
    t = S5_CHUNK
    hi = lax.Precision.HIGHEST
    step = jnp.exp(log_dt)[:, None]
    k = jnp.arange(t + 1, dtype=F32)[None, :, None]
    mag = jnp.exp(k * (a_re * step)[:, None, :])
    ang = k * (a_im * step)[:, None, :]
    pw_re, pw_im = mag * jnp.cos(ang), mag * jnp.sin(ang)
    ab_re, ab_im = pw_re[:, 1], pw_im[:, 1]
    den = a_re * a_re + a_im * a_im
    n_re, n_im = ab_re - 1.0, ab_im
    cf_re = (n_re * a_re + n_im * a_im) / den
    cf_im = (n_im * a_re - n_re * a_im) / den
    bb_re = cf_re[..., None] * b_re - cf_im[..., None] * b_im
    bb_im = cf_re[..., None] * b_im + cf_im[..., None] * b_re
    cp_re = c_re[:, None] * pw_re[:, :t, None, :] - c_im[:, None] * pw_im[:, :t, None, :]
    cp_im = c_re[:, None] * pw_im[:, :t, None, :] + c_im[:, None] * pw_re[:, :t, None, :]
    kern = (jnp.einsum('gshp,gpk->gshk', cp_re, bb_re, precision=hi)
            - jnp.einsum('gshp,gpk->gshk', cp_im, bb_im, precision=hi))
    g = a_re.shape[0]
    tg = S5_TILE_GROUPS
    nt = g // tg
    def tile_diag(m, pattern):
        m5 = jnp.einsum(pattern, m.reshape(nt, tg, t, m.shape[2], m.shape[3]).astype(BF16))
        r, c = m5.shape[3], m5.shape[4]
        rep = (jnp.arange(c)[:, None] == jnp.arange(tg * c)[None, :] % c).astype(BF16)
        out = jnp.einsum('qsrc,cn->qsrn', m5.reshape(nt, t, tg * r, c), rep,
                         preferred_element_type=BF16)
        on_diag = (jnp.arange(tg * r)[:, None] // r) == (jnp.arange(tg * c)[None, :] // c)
        return jnp.where(on_diag, out, jnp.zeros((), BF16))

    k_bd = tile_diag(kern, 'qgsab->qsgba')
    k_rev = (t - 1) - jnp.arange(t, dtype=F32)[None, :, None]
    mag_r = jnp.exp(k_rev * (a_re * step)[:, None, :])
    ang_r = k_rev * (a_im * step)[:, None, :]
    rp_re, rp_im = mag_r * jnp.cos(ang_r), mag_r * jnp.sin(ang_r)
    bt_re, bt_im = bb_re.transpose(0, 2, 1)[:, None], bb_im.transpose(0, 2, 1)[:, None]
    w_re = rp_re[:, :, None, :] * bt_re - rp_im[:, :, None, :] * bt_im
    w_im = rp_re[:, :, None, :] * bt_im + rp_im[:, :, None, :] * bt_re
    w_bd = jnp.concatenate([tile_diag(w_re, 'qgjhp->qjghp'),
                            tile_diag(w_im, 'qgjhp->qjghp')], axis=-1)
    q_re, q_im = pw_re[:, 1:], pw_im[:, 1:]
    v_re = c_re[:, None] * q_re[:, :, None, :] - c_im[:, None] * q_im[:, :, None, :]
    v_im = c_re[:, None] * q_im[:, :, None, :] + c_im[:, None] * q_re[:, :, None, :]
    v_bd = jnp.concatenate([tile_diag(v_re, 'qgihp->qigph'),
                            tile_diag(-v_im, 'qgihp->qigph')], axis=2)
    at_re = pw_re[:, t].reshape(nt, 1, tg * S5_STATE)
    at_im = pw_im[:, t].reshape(nt, 1, tg * S5_STATE)
    w_pair = w_bd.reshape(nt, t // 2, 2 * LANES, w_bd.shape[-1])
    k_all = k_bd.transpose(0, 2, 1, 3).reshape(nt, LANES, t * LANES)
    k_lag = jnp.concatenate([jnp.zeros((nt, LANES, LANES), BF16), k_all[:, :, :-LANES]], axis=2)
    k_pair = jnp.concatenate([k_all, k_lag], axis=1)
    v_all = v_bd.transpose(0, 2, 1, 3).reshape(nt, v_bd.shape[2], t * LANES)
    return w_pair, k_pair, v_all, at_re, at_im


def _outproj_kernel(ym_ref, ys_ref, yw_ref, y5_ref, u_ref, g5_ref, x_ref, w_ref, pg_ref,
                    d5_ref, gw_ref, gb_ref, o_ref):
    y5 = y5_ref[...] + d5_ref[...] * u_ref[...]
    c0 = math.sqrt(2.0 / math.pi)
    y5 = y5 * (0.5 * (1.0 + jnp.tanh(c0 * (y5 + 0.044715 * (y5 * y5 * y5)))))
    gl = jnp.dot(y5.astype(BF16), gw_ref[...], preferred_element_type=F32) + gb_ref[...]
    y5 = y5 * _sigmoid(gl) * _silu(g5_ref[...])
    parts = [ym_ref[...], ys_ref[...], yw_ref[...], y5.astype(BF16)]
    acc = None
    for i, part in enumerate(parts):
        d = jnp.dot(part, w_ref[i * PROJ_TILE:(i + 1) * PROJ_TILE, :], preferred_element_type=F32)
        acc = d if acc is None else acc + d
    ms = jnp.mean(acc * acc, axis=-1, keepdims=True)
    o_ref[...] = x_ref[...] + acc * lax.rsqrt(ms + NORM_EPS) * pg_ref[...]


def _outproj(y_moba, y_ssd, y_swa, y_s5, proj, x2, w_out, post_g, s5_d, glu_w, glu_b, layer, tm):
    m = x2.shape[0]
    rows = lambda w: pl.BlockSpec((tm, w), lambda i: (i, 0))
    full = lambda shape: pl.BlockSpec(shape, lambda i: (0, 0))
    of_layer = lambda shape: pl.BlockSpec((None,) + shape, lambda i: (layer, 0, 0))
    return pl.pallas_call(
        _outproj_kernel,
        grid=(m // tm,),
        in_specs=[
            rows(MOBA_W), rows(SSD_W), rows(SWA_W), rows(S5_W),
            pl.BlockSpec((tm, PROJ_TILE), lambda i: (i, T_SU)),
            pl.BlockSpec((tm, PROJ_TILE), lambda i: (i, T_S5G)),
            rows(D_MODEL),
            of_layer((MIX_W, D_MODEL)),
            full((1, D_MODEL)),
            full((1, S5_W)),
            of_layer((S5_W, S5_W)),
            full((1, S5_W)),
        ],
        out_specs=rows(D_MODEL),
        out_shape=jax.ShapeDtypeStruct((m, D_MODEL), F32),
        compiler_params=pltpu.CompilerParams(
            dimension_semantics=("parallel",),
            vmem_limit_bytes=VMEM_LIMIT_BYTES),
        name="outproj",
    )(y_moba, y_ssd, y_swa, y_s5, proj, proj, x2, w_out, post_g, s5_d, glu_w, glu_b)


def _swa_tiles(w, axis):
    s = w.shape
    per_kv = SWA_HEADS // SWA_KV_HEADS
    w = w.reshape(s[:axis] + (SWA_KV_HEADS, per_kv, HEAD_DIM) + s[axis + 1:])
    return jnp.swapaxes(w, axis, axis + 1).reshape(s)


def _rope_tables(seq):
    inv = 1.0 / (ROPE_THETA ** (jnp.arange(0, HEAD_DIM, 2, dtype=F32) / HEAD_DIM))
    ang = jnp.arange(seq, dtype=F32)[:, None] * inv[None, :]
    cos, sin = jnp.cos(ang), jnp.sin(ang)
    return (jnp.concatenate([cos, cos, cos, cos], axis=1),
            jnp.concatenate([-sin, -sin, sin, sin], axis=1))


(IN_MQ, IN_MK, IN_MV, IN_MG, IN_XBC, IN_DT, IN_Z, IN_SQ, IN_SK, IN_SV, IN_SG, IN_SU, IN_S5G, IN_W) = (
    int(v) for v in np.cumsum([0, MOBA_W, MOBA_W, MOBA_W, MOBA_W, SSD_CONV_CH, SSD_HEADS, SSD_W,
                               SWA_W, SWA_KV_W, SWA_KV_W, SWA_W, S5_W, S5_W]))
WPREP_COLS = 256


def _wprep_moves():
    hx = HEAD_DIM // 2
    moves = []

    def straight(dst, src, n):
        moves.append((dst, src, n))

    def pair_tile(dst, head_a, head_b):
        for q, src in enumerate((head_a, head_b, head_a + hx, head_b + hx)):
            moves.append((dst + q * hx, src, hx))

    per = PROJ_TILE // LANES
    per_kv = SWA_HEADS // SWA_KV_HEADS
    for i in range(per):
        pair_tile(T_MQ * PROJ_TILE + i * LANES, IN_MQ + i * LANES, IN_MQ + i * LANES + HEAD_DIM)
        pair_tile(T_MK * PROJ_TILE + i * LANES, IN_MK + i * LANES, IN_MK + i * LANES + HEAD_DIM)
        pair_tile(T_SQ * PROJ_TILE + i * LANES, IN_SQ + i * HEAD_DIM, IN_SQ + (i + per_kv) * HEAD_DIM)
        straight(T_SG * PROJ_TILE + i * LANES, IN_SG + i * HEAD_DIM, HEAD_DIM)
        straight(T_SG * PROJ_TILE + i * LANES + HEAD_DIM, IN_SG + (i + per_kv) * HEAD_DIM, HEAD_DIM)
    straight(T_MV * PROJ_TILE, IN_MV, MOBA_W)
    straight(T_MG * PROJ_TILE, IN_MG, MOBA_W)
    straight(T_XBC * PROJ_TILE, IN_XBC, SSD_CONV_CH)
    straight(T_Z * PROJ_TILE, IN_Z, SSD_W)
    straight(T_SU * PROJ_TILE, IN_SU, S5_W)
    straight(T_S5G * PROJ_TILE, IN_S5G, S5_W)
    pair_tile(TAIL_K_BLK * LANES, IN_SK, IN_SK + HEAD_DIM)
    straight(TAIL_V_BLK * LANES, IN_SV, SWA_KV_W)
    return moves


def _wprep_kernel(w_ref, o_ref):
    for dst, src, n in _wprep_moves():
        o_ref[dst:dst + n, :] = w_ref[src:src + n, :].astype(BF16)
    dt0 = TAIL_DT_BLK * LANES
    pad_rows = PROJ_W - dt0 - SSD_HEADS
    o_ref[dt0:, :] = jnp.concatenate(
        [w_ref[IN_DT:IN_DT + SSD_HEADS, :], jnp.zeros((pad_rows, w_ref.shape[1]), F32)],
        axis=0).astype(BF16)


def _cat_w_in(w_in):
    w_t = jnp.swapaxes(w_in, 1, 2)
    depth, _, d = w_t.shape
    tc = WPREP_COLS
    return pl.pallas_call(
        _wprep_kernel,
        grid=(depth, d // tc),
        in_specs=[pl.BlockSpec((None, IN_W, tc), lambda l, c: (l, 0, c))],
        out_specs=pl.BlockSpec((None, PROJ_W, tc), lambda l, c: (l, 0, c)),
        out_shape=jax.ShapeDtypeStruct((depth, PROJ_W, d), BF16),
        compiler_params=pltpu.CompilerParams(
            dimension_semantics=("parallel", "parallel"),
            vmem_limit_bytes=VMEM_LIMIT_BYTES),
        name="wprep",
    )(w_t)


def _pad_lanes(v):
    return jnp.pad(v, (0, LANES - v.shape[0]))[None, :]


def _layer(layer, x2, b, seq, cos_t, sin_t, pre_g, post_g, w_in, w_out, conv_w, conv_b, dt_bias, a_log,
           ssd_d, ssd_norm, sinks, s5_tabs, s5_d, glu_w, glu_b):
    proj = _inproj(x2, pre_g[None, :], w_in, layer, cos_t, sin_t, seq, min(TM_INPROJ, seq))
    proj3 = proj.reshape(b, seq, PROJ_W)

    y_moba = _moba(proj3)
    y_swa = _swa(proj3, sinks)
    y_ssd = _ssd(proj3, conv_w, conv_b[None, :], _pad_lanes(dt_bias), _pad_lanes(a_log),
                 jnp.repeat(ssd_d, SSD_HEAD_DIM)[None, :], ssd_norm[None, :])
    y_s5 = _s5_core(proj3, *s5_tabs, layer).reshape(b * seq, S5_W)

    m = b * seq
    return _outproj(y_moba.reshape(m, MOBA_W), y_ssd.reshape(m, SSD_W), y_swa.reshape(m, SWA_W),
                    y_s5, proj, x2, w_out, post_g[None, :], s5_d[None, :],
                    glu_w, glu_b[None, :], layer, min(TM_OUTPROJ, seq))


def kernel(x, pre_norm, post_norm, w_in, w_out, ssd_conv_w, ssd_conv_b, ssd_dt_bias, ssd_a_log, ssd_d, ssd_norm, swa_sinks, s5_a_re, s5_a_im, s5_log_dt, s5_b_re, s5_b_im, s5_c_re, s5_c_im, s5_d, s5_glu_w, s5_glu_b):
    b, seq, d = x.shape
    assert d == D_MODEL
    assert seq % MOBA_BLOCK == 0 and seq % SSD_CHUNK == 0 and seq % SWA_WINDOW == 0
    assert seq % (S5_CHUNK * S5_SCAN_ROWS) == 0
    assert seq % SWA_TQ == 0
    depth = pre_norm.shape[0]
    cos_t, sin_t = _rope_tables(seq)
    w_cat = _cat_w_in(w_in)
    swa_lo = MOBA_W + SSD_W
    w_out_p = jnp.concatenate([w_out[:, :swa_lo], _swa_tiles(w_out[:, swa_lo:swa_lo + SWA_W], 1),
                               w_out[:, swa_lo + SWA_W:]], axis=1).astype(BF16)
    per_kv = SWA_HEADS // SWA_KV_HEADS
    sinks_p = jnp.swapaxes(swa_sinks.reshape(depth, SWA_KV_HEADS, per_kv), 1, 2).reshape(depth, SWA_HEADS)
    glu_w16 = s5_glu_w.astype(BF16)
    fold = lambda a: a.reshape((depth * S5_GROUPS,) + a.shape[2:])
    tabs = _s5_tables(fold(s5_a_re), fold(s5_a_im), s5_log_dt.reshape(-1), fold(s5_b_re), fold(s5_b_im),
                      fold(s5_c_re), fold(s5_c_im))

    x2 = x.reshape(b * seq, d)
    for l in range(depth):
        x2 = _layer(l, x2, b, seq, cos_t, sin_t, pre_norm[l], post_norm[l], w_cat, w_out_p,
                    ssd_conv_w[l], ssd_conv_b[l], ssd_dt_bias[l], ssd_a_log[l], ssd_d[l], ssd_norm[l],
                    sinks_p[l], tabs, s5_d[l], glu_w16, s5_glu_b[l])
    return x2.reshape(b, seq, d)
```

```python
import functools
import math

import jax
import jax.numpy as jnp
import numpy as np
from jax import lax
from jax.experimental import pallas as pl
from jax.experimental.pallas import tpu as pltpu

F32 = jnp.float32
BF16 = jnp.bfloat16

D_MODEL = 2048
HEAD_DIM = 64
ROPE_THETA = 10000.0
NORM_EPS = 1e-6
MOBA_HEADS = 8
MOBA_W = MOBA_HEADS * HEAD_DIM
MOBA_BLOCK = 256
MOBA_TOPK = 3
SSD_HEADS = 8
SSD_HEAD_DIM = 64
SSD_W = SSD_HEADS * SSD_HEAD_DIM
SSD_GROUPS = 2
SSD_STATE = 128
SSD_CONV = 4
SSD_CHUNK = 256
SSD_CONV_CH = SSD_W + 2 * SSD_GROUPS * SSD_STATE
SWA_HEADS = 8
SWA_KV_HEADS = 2
SWA_W = SWA_HEADS * HEAD_DIM
SWA_KV_W = SWA_KV_HEADS * HEAD_DIM
SWA_WINDOW = 128
S5_W = 512
S5_GROUP = 16
S5_GROUPS = S5_W // S5_GROUP
S5_STATE = 64
MIX_W = MOBA_W + SSD_W + SWA_W + S5_W

LANES = 128
SUBLANES = 8
VMEM_LIMIT_BYTES = 48 * 1024 * 1024

PROJ_TILE = 512
T_MQ, T_MK, T_MV, T_MG, T_XBC, T_Z, T_SG, T_SQ, T_TAIL, T_SU, T_S5G = 0, 1, 2, 3, 4, 6, 7, 8, 9, 10, 11
PROJ_W = 12 * PROJ_TILE
INPROJ_TN = 2 * PROJ_TILE
assert T_MK == T_MQ + 1 and T_MQ % 2 == 0 and T_TAIL == T_SQ + 1 and T_SQ % 2 == 0
TAIL_K_BLK = T_TAIL * (PROJ_TILE // LANES)
TAIL_V_BLK = TAIL_K_BLK + 1
TAIL_DT_BLK = TAIL_K_BLK + 2

NEG = -1e30
TM_INPROJ = 1024
TM_OUTPROJ = 512
SWA_TQ = 512
MOBA_PAIRS = 4
MOBA_ONES_ROWS = 16
S5_CHUNK = 8
S5_TILE_GROUPS = LANES // S5_GROUP
S5_SCAN_ROWS = SUBLANES


def _silu(x):
    return x / (1.0 + jnp.exp(-x))


def _sigmoid(x):
    return 1.0 / (1.0 + jnp.exp(-x))


def _bf16_parts(x):
    hi = x.astype(BF16)
    r1 = x - hi.astype(F32)
    mid = r1.astype(BF16)
    lo = (r1 - mid.astype(F32)).astype(BF16)
    return hi, mid, lo


def _nt_dot(a, b, precision=None):
    return lax.dot_general(a, b, (((1,), (1,)), ((), ())), precision=precision,
                           preferred_element_type=F32)


def _tn_dot(a, b, precision=None):
    return lax.dot_general(a, b, (((0,), (0,)), ((), ())), precision=precision,
                           preferred_element_type=F32)


def _inproj_kernel(x_ref, g_ref, w_ref, cos_ref, sin_ref, o_ref, h_scr):
    j = pl.program_id(1)

    @pl.when(j == 0)
    def _():
        x = x_ref[...]
        ms = jnp.mean(x * x, axis=-1, keepdims=True)
        h_scr[...] = (x * lax.rsqrt(ms + NORM_EPS) * g_ref[...]).astype(BF16)

    acc = _nt_dot(h_scr[...], w_ref[...])
    j_qk = T_MQ * PROJ_TILE // INPROJ_TN
    j_swa = T_SQ * PROJ_TILE // INPROJ_TN

    def store(n_rope):
        cos, sin = cos_ref[...], sin_ref[...]
        for t in range(acc.shape[1] // LANES):
            a = acc[:, t * LANES:(t + 1) * LANES]
            if t < n_rope:
                a = a * cos + pltpu.roll(a, LANES // 2, axis=1) * sin
            o_ref[:, t * LANES:(t + 1) * LANES] = a

    @pl.when(j == j_qk)
    def _():
        store(INPROJ_TN // LANES)

    @pl.when(j == j_swa)
    def _():
        store((SWA_W + SWA_KV_W) // LANES)

    @pl.when((j != j_qk) & (j != j_swa))
    def _():
        o_ref[...] = acc


def _inproj(x2, pre_g, w_cat, layer, cos_t, sin_t, seq, tm):
    m = x2.shape[0]
    nseq = seq // tm
    return pl.pallas_call(
        _inproj_kernel,
        grid=(m // tm, PROJ_W // INPROJ_TN),
        in_specs=[
            pl.BlockSpec((tm, D_MODEL), lambda i, j: (i, 0)),
            pl.BlockSpec((1, D_MODEL), lambda i, j: (0, 0)),
            pl.BlockSpec((None, INPROJ_TN, D_MODEL), lambda i, j: (layer, j, 0)),
            pl.BlockSpec((tm, LANES), lambda i, j: (i % nseq, 0)),
            pl.BlockSpec((tm, LANES), lambda i, j: (i % nseq, 0)),
        ],
        out_specs=pl.BlockSpec((tm, INPROJ_TN), lambda i, j: (i, j)),
        out_shape=jax.ShapeDtypeStruct((m, PROJ_W), F32),
        scratch_shapes=[pltpu.VMEM((tm, D_MODEL), BF16)],
        compiler_params=pltpu.CompilerParams(
            dimension_semantics=("parallel", "arbitrary"),
            vmem_limit_bytes=VMEM_LIMIT_BYTES),
        name="inproj",
    )(x2, pre_g, w_cat, cos_t, sin_t)


def _pair_head(idx):
    return (idx // (HEAD_DIM // 2)) % 2


def _moba_kernel(q_ref, k_ref, v_ref, g_ref, o_ref, kaug_scr, vt_scr, kmean_scr, qaug_scr,
                 acc_scr, sa_scr, sb_scr, *, nb):
    qi = pl.program_id(2)
    blk = MOBA_BLOCK
    qscale = (HEAD_DIM ** -0.5) * math.log2(math.e)

    nh = 2 * MOBA_PAIRS

    @pl.when(qi == 0)
    def _():
        seq = k_ref.shape[0]
        row_blk = lax.broadcasted_iota(jnp.int32, (seq, LANES), 0) // blk
        lane = lax.broadcasted_iota(jnp.int32, (seq, LANES), 1)
        onehot = jnp.where(row_blk == lane, 1.0, 0.0).astype(BF16)
        kmean_scr[...] = jnp.mean(k_ref[...].reshape(nb, blk, MOBA_PAIRS * LANES), axis=1)
        for pp in range(MOBA_PAIRS):
            kaug_scr[pp, :, :LANES] = k_ref[:, pp * LANES:(pp + 1) * LANES].astype(BF16)
            kaug_scr[pp, :, LANES:] = onehot
        ones = jnp.ones((MOBA_ONES_ROWS, blk), BF16)
        for n in range(nb):
            vt = v_ref[n * blk:(n + 1) * blk, :].T.astype(BF16)
            for hd in range(nh):
                vt_scr[hd, n, 0:HEAD_DIM, :] = vt[hd * HEAD_DIM:(hd + 1) * HEAD_DIM, :]
                vt_scr[hd, n, HEAD_DIM:, :] = ones

    q_all = q_ref[...].T
    sub = lax.broadcasted_iota(jnp.int32, (LANES, blk), 0)
    klane = lax.broadcasted_iota(jnp.int32, (nb, LANES), 1)
    blk_id = lax.broadcasted_iota(jnp.int32, (nb, blk), 0)
    past = blk_id < qi

    for hd in range(nh):
        pp, hh = divmod(hd, 2)
        q_t = q_all[pp * LANES:(pp + 1) * LANES]
        km = jnp.where(_pair_head(klane) == hh, kmean_scr[:, pp * LANES:(pp + 1) * LANES], 0.0)
        gate = jnp.dot(km, q_t, precision=lax.Precision.HIGHEST, preferred_element_type=F32)
        gm = jnp.where(past, gate, -jnp.inf)
        cnt = jnp.zeros((nb, blk), F32)
        for n in range(nb):
            gn = gm[n:n + 1, :]
            beats = (gn > gm) | ((gn == gm) & (blk_id > n))
            cnt = cnt + jnp.where(beats & (qi > n), 1.0, 0.0)
        keep = (past & (cnt < float(MOBA_TOPK))) | (blk_id == qi)
        pen = jnp.where(keep, 0.0, NEG)
        pen = jnp.concatenate([pen, jnp.zeros((LANES - nb, blk), F32)], axis=0)
        qh = jnp.where(_pair_head(sub) == hh, q_t, 0.0) * qscale
        qaug_scr[hd] = jnp.concatenate([qh.astype(BF16), pen.astype(BF16)], axis=0)
        acc_scr[hd] = jnp.zeros(acc_scr.shape[1:], F32)

    def qk(t, s_ref):
        start = pl.multiple_of(t * blk, blk)
        for pp in range(MOBA_PAIRS):
            kb = kaug_scr[pp, pl.ds(start, blk), :]
            for hd in (2 * pp, 2 * pp + 1):
                s_ref[hd] = jnp.dot(kb, qaug_scr[hd], preferred_element_type=F32)

    def process(t, s_ref, ms, tail):
        out = []
        for hd in range(nh):
            s = s_ref[hd]
            if tail:
                s = jnp.where(key_minus_qry <= (qi - t) * blk, s, NEG)
            m_new = jnp.maximum(ms[hd], jnp.max(s, axis=0, keepdims=True))
            alpha = jnp.exp2(ms[hd] - m_new)
            p = jnp.exp2(s - m_new).astype(BF16)
            pv = jnp.dot(vt_scr[hd, jnp.minimum(t, qi)], p, preferred_element_type=F32)
            acc_scr[hd] = alpha * acc_scr[hd] + pv
            out.append(m_new)
        return tuple(out)

    qk(0, sa_scr)

    def body(k, ms):
        t = 2 * k
        qk(t + 1, sb_scr)
        ms = process(t, sa_scr, ms, False)
        qk(t + 2, sa_scr)
        return process(t + 1, sb_scr, ms, False)

    m_init = jnp.full((1, blk), NEG, F32)
    ms = lax.fori_loop(0, qi // 2, body, (m_init,) * nh)
    key_minus_qry = (lax.broadcasted_iota(jnp.int32, (blk, blk), 0)
                     - lax.broadcasted_iota(jnp.int32, (blk, blk), 1))
    t1 = 2 * (qi // 2)
    has_second = t1 < qi

    @pl.when(has_second)
    def _():
        qk(t1 + 1, sb_scr)
        process(t1 + 1, sb_scr, process(t1, sa_scr, ms, False), True)

    @pl.when(jnp.logical_not(has_second))
    def _():
        process(t1, sa_scr, ms, True)

    outs = [acc_scr[hd][:HEAD_DIM] / acc_scr[hd][HEAD_DIM:HEAD_DIM + 1] for hd in range(nh)]
    out = jnp.concatenate(outs, axis=0).T
    o_ref[...] = (out * _silu(g_ref[...])).astype(o_ref.dtype)


def _moba(proj3):
    b, seq, _ = proj3.shape
    blk = MOBA_BLOCK
    nb = seq // blk
    wid = MOBA_PAIRS * LANES
    ngrp = MOBA_W // wid
    per = PROJ_TILE // wid
    nh = 2 * MOBA_PAIRS
    return pl.pallas_call(
        functools.partial(_moba_kernel, nb=nb),
        grid=(b, ngrp, nb),
        in_specs=[
            pl.BlockSpec((None, blk, wid), lambda bi, hp, qi: (bi, qi, T_MQ * per + hp)),
            pl.BlockSpec((None, seq, wid), lambda bi, hp, qi: (bi, 0, T_MK * per + hp),
                         pipeline_mode=pl.Buffered(1)),
            pl.BlockSpec((None, seq, wid), lambda bi, hp, qi: (bi, 0, T_MV * per + hp),
                         pipeline_mode=pl.Buffered(1)),
            pl.BlockSpec((None, blk, wid), lambda bi, hp, qi: (bi, qi, T_MG * per + hp)),
        ],
        out_specs=pl.BlockSpec((None, blk, wid), lambda bi, hp, qi: (bi, qi, hp)),
        out_shape=jax.ShapeDtypeStruct((b, seq, MOBA_W), BF16),
        scratch_shapes=[pltpu.VMEM((MOBA_PAIRS, seq, 2 * LANES), BF16),
                        pltpu.VMEM((nh, nb, HEAD_DIM + MOBA_ONES_ROWS, blk), BF16),
                        pltpu.VMEM((nb, wid), F32),
                        pltpu.VMEM((nh, 2 * LANES, blk), BF16),
                        pltpu.VMEM((nh, HEAD_DIM + MOBA_ONES_ROWS, blk), F32),
                        pltpu.VMEM((nh, blk, blk), F32),
                        pltpu.VMEM((nh, blk, blk), F32)],
        compiler_params=pltpu.CompilerParams(
            dimension_semantics=("parallel", "parallel", "arbitrary"),
            vmem_limit_bytes=VMEM_LIMIT_BYTES),
        name="moba",
    )(proj3, proj3, proj3, proj3)


def _swa_kernel(sink_ref, q_ref, kp_ref, kc_ref, vp_ref, vc_ref, g_ref, o_ref):
    n = pl.program_id(1)
    w = SWA_WINDOW
    scale = HEAD_DIM ** -0.5
    kcat = jnp.concatenate([kp_ref[...], kc_ref[...]], axis=0).astype(BF16)
    vcat = jnp.concatenate([vp_ref[...], vc_ref[...]], axis=0).astype(BF16)
    row = lax.broadcasted_iota(jnp.int32, (w, 2 * w), 0)
    col = lax.broadcasted_iota(jnp.int32, (w, 2 * w), 1)
    band = (col > row) & (col <= row + w)
    lane = lax.broadcasted_iota(jnp.int32, (w, LANES), 1)
    for win in range(SWA_TQ // w):
        valid = band & ((n > 0) | (col >= w)) if win == 0 else band
        kw = kcat[win * w:(win + 2) * w]
        vw = vcat[win * w:(win + 2) * w]
        rows = slice(win * w, (win + 1) * w)
        for i in range(SWA_W // LANES):
            q_t = q_ref[rows, i * LANES:(i + 1) * LANES]
            halves = []
            for c in range(2):
                qm = (jnp.where(_pair_head(lane) == c, q_t, 0.0) * scale).astype(BF16)
                s = _nt_dot(qm, kw)
                s = jnp.where(valid, s, NEG)
                sink = sink_ref[2 * i + c]
                m = jnp.maximum(jnp.max(s, axis=-1, keepdims=True), sink)
                e = jnp.exp(s - m)
                denom = jnp.sum(e, axis=-1, keepdims=True) + jnp.exp(sink - m)
                o = jnp.dot(e.astype(BF16), vw, preferred_element_type=F32)
                halves.append(o / denom)
            out = jnp.where(lane < HEAD_DIM, halves[0], halves[1])
            g_t = g_ref[rows, i * LANES:(i + 1) * LANES]
            o_ref[rows, i * LANES:(i + 1) * LANES] = (out * _silu(g_t)).astype(o_ref.dtype)


def _swa(proj3, sinks_perm):
    b, seq, _ = proj3.shape
    w = SWA_WINDOW
    tq = SWA_TQ
    per = tq // w
    prev = lambda bi, n: (bi, jnp.maximum(n * per - 1, 0), TAIL_K_BLK)
    prev_v = lambda bi, n: (bi, jnp.maximum(n * per - 1, 0), TAIL_V_BLK)
    return pl.pallas_call(
        _swa_kernel,
        grid=(b, seq // tq),
        in_specs=[
            pl.BlockSpec(memory_space=pltpu.SMEM),
            pl.BlockSpec((None, tq, SWA_W), lambda bi, n: (bi, n, T_SQ)),
            pl.BlockSpec((None, w, LANES), prev),
            pl.BlockSpec((None, tq, LANES), lambda bi, n: (bi, n, TAIL_K_BLK)),
            pl.BlockSpec((None, w, LANES), prev_v),
            pl.BlockSpec((None, tq, LANES), lambda bi, n: (bi, n, TAIL_V_BLK)),
            pl.BlockSpec((None, tq, SWA_W), lambda bi, n: (bi, n, T_SG)),
        ],
        out_specs=pl.BlockSpec((None, tq, SWA_W), lambda bi, n: (bi, n, 0)),
        out_shape=jax.ShapeDtypeStruct((b, seq, SWA_W), BF16),
        compiler_params=pltpu.CompilerParams(
            dimension_semantics=("parallel", "arbitrary"),
            vmem_limit_bytes=VMEM_LIMIT_BYTES),
        name="swa",
    )(sinks_perm, proj3, proj3, proj3, proj3, proj3, proj3)


def _ssd_kernel(xbc_ref, z_ref, dt_ref, cw_ref, cb_ref, dtb_ref, alog_ref, dsk_ref, nw_ref,
                o_ref, xbuf, state):
    c = pl.program_id(1)
    s = SSD_CHUNK
    pad = SUBLANES
    hi = lax.Precision.HIGHEST

    @pl.when(c == 0)
    def _():
        xbuf[0:pad, :] = jnp.zeros((pad, SSD_CONV_CH), F32)
        state[...] = jnp.zeros_like(state)

    @pl.when(c > 0)
    def _():
        xbuf[0:pad, :] = xbuf[s:s + pad, :]

    xbuf[pad:pad + s, :] = xbc_ref[...]
    conv = cb_ref[...]
    for k in range(SSD_CONV):
        off = pad - (SSD_CONV - 1) + k
        conv = conv + cw_ref[k:k + 1, :] * xbuf[off:off + s, :]
    xbc = _silu(conv)
    xs = xbc[:, :SSD_W]
    gw = SSD_GROUPS * SSD_STATE
    bm = xbc[:, SSD_W:SSD_W + gw]
    cm = xbc[:, SSD_W + gw:]

    dtr = dt_ref[...] + dtb_ref[...]
    dt = jnp.maximum(dtr, 0.0) + jnp.log(1.0 + jnp.exp(-jnp.abs(dtr)))
    a = dt * (-jnp.exp(alog_ref[...]))
    row = lax.broadcasted_iota(jnp.int32, (s, s), 0)
    col = lax.broadcasted_iota(jnp.int32, (s, s), 1)
    lower = row >= col
    tri = jnp.where(lower, 1.0, 0.0).astype(BF16)
    acum = sum(jnp.dot(tri, part, preferred_element_type=F32) for part in _bf16_parts(a))
    acum_t = acum.T
    alast = acum[s - 1:s, :]

    erow = lax.broadcasted_iota(jnp.int32, (LANES, SSD_W), 0)
    ecol = lax.broadcasted_iota(jnp.int32, (LANES, SSD_W), 1)
    expand = jnp.where(erow == ecol // SSD_HEAD_DIM, 1.0, 0.0).astype(BF16)
    cols = jnp.concatenate([dt, jnp.exp(alast - acum), jnp.exp(acum),
                            jnp.broadcast_to(jnp.exp(alast), (SUBLANES, LANES))], axis=0)
    cols_e = sum(jnp.dot(part, expand, preferred_element_type=F32) for part in _bf16_parts(cols))
    dt_e, dec_e, ea_e = cols_e[:s], cols_e[s:2 * s], cols_e[2 * s:3 * s]
    cd_e = cols_e[3 * s:3 * s + 1]

    x_dt = xs * dt_e
    x_dec = (x_dt * dec_e).astype(BF16)
    x_dt16 = x_dt.astype(BF16)
    lane = lax.broadcasted_iota(jnp.int32, (s, LANES), 1)
    hpg = SSD_HEADS // SSD_GROUPS
    gwid = hpg * SSD_HEAD_DIM
    y_parts = []
    for g in range(SSD_GROUPS):
        bg = bm[:, g * SSD_STATE:(g + 1) * SSD_STATE].astype(BF16)
        cg = cm[:, g * SSD_STATE:(g + 1) * SSD_STATE].astype(BF16)
        cbm = _nt_dot(cg, bg)
        st = state[:, g * gwid:(g + 1) * gwid]
        y_off = jnp.dot(cg, st.astype(BF16), preferred_element_type=F32) \
            * ea_e[:, g * gwid:(g + 1) * gwid]
        new_st = cd_e[:, g * gwid:(g + 1) * gwid] * st + _tn_dot(bg, x_dec[:, g * gwid:(g + 1) * gwid])
        state[:, g * gwid:(g + 1) * gwid] = new_st
        for i in range(hpg // 2):
            lo = g * gwid + i * LANES
            xp = x_dt16[:, lo:lo + LANES]
            ys = []
            for hh in range(2):
                h = g * hpg + 2 * i + hh
                seg = acum[:, h:h + 1] - acum_t[h:h + 1, :]
                lm = jnp.where(lower, jnp.exp(jnp.minimum(seg, 0.0)), 0.0)
                ys.append(jnp.dot((cbm * lm).astype(BF16), xp, preferred_element_type=F32))
            y_parts.append(jnp.where(lane < SSD_HEAD_DIM, ys[0], ys[1])
                           + y_off[:, i * LANES:(i + 1) * LANES])
    y = jnp.concatenate(y_parts, axis=1)
    y = (y + dsk_ref[...] * xs) * _silu(z_ref[...])
    outs = []
    for g in range(SSD_GROUPS):
        yg = y[:, g * gwid:(g + 1) * gwid]
        outs.append(yg * lax.rsqrt(jnp.mean(yg * yg, axis=-1, keepdims=True) + NORM_EPS))
    o_ref[...] = (jnp.concatenate(outs, axis=1) * nw_ref[...]).astype(o_ref.dtype)


def _ssd(proj3, conv_w, conv_b, dt_bias_p, a_log_p, d_skip_e, norm_w):
    b, seq, _ = proj3.shape
    s = SSD_CHUNK
    full = lambda shape: pl.BlockSpec(shape, lambda bi, c: (0, 0))
    return pl.pallas_call(
        _ssd_kernel,
        grid=(b, seq // s),
        in_specs=[
            pl.BlockSpec((None, s, SSD_CONV_CH), lambda bi, c: (bi, c, T_XBC * PROJ_TILE // SSD_CONV_CH)),
            pl.BlockSpec((None, s, SSD_W), lambda bi, c: (bi, c, T_Z)),
            pl.BlockSpec((None, s, LANES), lambda bi, c: (bi, c, TAIL_DT_BLK)),
            full((SSD_CONV, SSD_CONV_CH)),
            full((1, SSD_CONV_CH)),
            full((1, LANES)),
            full((1, LANES)),
            full((1, SSD_W)),
            full((1, SSD_W)),
        ],
        out_specs=pl.BlockSpec((None, s, SSD_W), lambda bi, c: (bi, c, 0)),
        out_shape=jax.ShapeDtypeStruct((b, seq, SSD_W), BF16),
        scratch_shapes=[pltpu.VMEM((s + 2 * SUBLANES, SSD_CONV_CH), F32),
                        pltpu.VMEM((SSD_STATE, SSD_W), F32)],
        compiler_params=pltpu.CompilerParams(
            dimension_semantics=("parallel", "arbitrary"),
            vmem_limit_bytes=VMEM_LIMIT_BYTES),
        name="ssd",
    )(proj3, proj3, proj3, conv_w, conv_b, dt_bias_p, a_log_p, d_skip_e, norm_w)


def _s5_kernel(u_ref, w_ref, k_ref, v_ref, are_ref, aim_ref, y_ref, x_scr, e_scr, s_scr, y_scr,
               *, nrow):
    t = S5_CHUNK
    half = S5_TILE_GROUPS * S5_STATE
    for j in range(t):
        x_scr[j // 2, :, (j % 2) * LANES:(j % 2 + 1) * LANES] = (
            u_ref[pl.ds(j, nrow, stride=t), :].astype(BF16))
    e = jnp.dot(x_scr[0], w_ref[0], preferred_element_type=F32)
    for jp in range(1, t // 2):
        e = e + jnp.dot(x_scr[jp], w_ref[jp], preferred_element_type=F32)
    e_scr[...] = e
    a_re = are_ref[...]
    a_im = aim_ref[...]

    def body(it, carry):
        s_re, s_im = carry
        r0 = pl.multiple_of(it * S5_SCAN_ROWS, S5_SCAN_ROWS)
        e8 = e_scr[pl.ds(r0, S5_SCAN_ROWS), :]
        rows = []
        for kk in range(S5_SCAN_ROWS):
            rows.append(jnp.concatenate([s_re, s_im], axis=1))
            e_re, e_im = e8[kk:kk + 1, :half], e8[kk:kk + 1, half:]
            s_re, s_im = a_re * s_re - a_im * s_im + e_re, a_re * s_im + a_im * s_re + e_im
        s_scr[pl.ds(r0, S5_SCAN_ROWS), :] = jnp.concatenate(rows, axis=0)
        return s_re, s_im

    zero = jnp.zeros((1, half), F32)
    lax.fori_loop(0, nrow // S5_SCAN_ROWS, body, (zero, zero))
    y_scr[...] = jnp.dot(s_scr[...].astype(BF16), v_ref[...], preferred_element_type=F32)
    for jp in range(t // 2):
        j = 2 * jp
        y_scr[:, j * LANES:] += jnp.dot(x_scr[jp], k_ref[:, :(t - j) * LANES],
                                        preferred_element_type=F32)
    for i in range(t):
        y_ref[pl.ds(i, nrow, stride=t), :] = y_scr[:, i * LANES:(i + 1) * LANES]


def _s5_core(proj3, w, k, v, a_re, a_im, layer):
    b, seq, _ = proj3.shape
    t = S5_CHUNK
    nrow = seq // t
    ntile = S5_W // LANES
    half = S5_TILE_GROUPS * S5_STATE
    per = PROJ_TILE // LANES
    base = layer * ntile
    return pl.pallas_call(
        functools.partial(_s5_kernel, nrow=nrow),
        grid=(ntile, b),
        in_specs=[
            pl.BlockSpec((None, seq, LANES), lambda q, bi: (bi, 0, T_SU * per + q)),
            pl.BlockSpec((None, t // 2, 2 * LANES, 2 * half), lambda q, bi: (base + q, 0, 0, 0)),
            pl.BlockSpec((None, 2 * LANES, t * LANES), lambda q, bi: (base + q, 0, 0)),
            pl.BlockSpec((None, 2 * half, t * LANES), lambda q, bi: (base + q, 0, 0)),
            pl.BlockSpec((None, 1, half), lambda q, bi: (base + q, 0, 0)),
            pl.BlockSpec((None, 1, half), lambda q, bi: (base + q, 0, 0)),
        ],
        out_specs=pl.BlockSpec((None, seq, LANES), lambda q, bi: (bi, 0, q)),
        out_shape=jax.ShapeDtypeStruct((b, seq, S5_W), F32),
        scratch_shapes=[pltpu.VMEM((t // 2, nrow, 2 * LANES), BF16),
                        pltpu.VMEM((nrow, 2 * half), F32),
                        pltpu.VMEM((nrow, 2 * half), F32),
                        pltpu.VMEM((nrow, t * LANES), F32)],
        compiler_params=pltpu.CompilerParams(
            dimension_semantics=("parallel", "parallel"),
            vmem_limit_bytes=VMEM_LIMIT_BYTES),
        name="s5",
    )(proj3, w, k, v, a_re, a_im)


def _s5_tables(a_re, a_im, log_dt, b_re, b_im, c_re, c_im):
    t = S5_CHUNK
    hi = lax.Precision.HIGHEST
    step = jnp.exp(log_dt)[:, None]
    k = jnp.arange(t + 1, dtype=F32)[None, :, None]
    mag = jnp.exp(k * (a_re * step)[:, None, :])
    ang = k * (a_im * step)[:, None, :]
    pw_re, pw_im = mag * jnp.cos(ang), mag * jnp.sin(ang)
    ab_re, ab_im = pw_re[:, 1], pw_im[:, 1]
    den = a_re * a_re + a_im * a_im
    n_re, n_im = ab_re - 1.0, ab_im
    cf_re = (n_re * a_re + n_im * a_im) / den
    cf_im = (n_im * a_re - n_re * a_im) / den
    bb_re = cf_re[..., None] * b_re - cf_im[..., None] * b_im
    bb_im = cf_re[..., None] * b_im + cf_im[..., None] * b_re
    cp_re = c_re[:, None] * pw_re[:, :t, None, :] - c_im[:, None] * pw_im[:, :t, None, :]
    cp_im = c_re[:, None] * pw_im[:, :t, None, :] + c_im[:, None] * pw_re[:, :t, None, :]
    kern = (jnp.einsum('gshp,gpk->gshk', cp_re, bb_re, precision=hi)
            - jnp.einsum('gshp,gpk->gshk', cp_im, bb_im, precision=hi))
    g = a_re.shape[0]
    tg = S5_TILE_GROUPS
    nt = g // tg
    def tile_diag(m, pattern):
        m5 = jnp.einsum(pattern, m.reshape(nt, tg, t, m.shape[2], m.shape[3]).astype(BF16))
        r, c = m5.shape[3], m5.shape[4]
        rep = (jnp.arange(c)[:, None] == jnp.arange(tg * c)[None, :] % c).astype(BF16)
        out = jnp.einsum('qsrc,cn->qsrn', m5.reshape(nt, t, tg * r, c), rep,
                         preferred_element_type=BF16)
        on_diag = (jnp.arange(tg * r)[:, None] // r) == (jnp.arange(tg * c)[None, :] // c)
        return jnp.where(on_diag, out, jnp.zeros((), BF16))

    k_bd = tile_diag(kern, 'qgsab->qsgba')
    k_rev = (t - 1) - jnp.arange(t, dtype=F32)[None, :, None]
    mag_r = jnp.exp(k_rev * (a_re * step)[:, None, :])
    ang_r = k_rev * (a_im * step)[:, None, :]
    rp_re, rp_im = mag_r * jnp.cos(ang_r), mag_r * jnp.sin(ang_r)
    bt_re, bt_im = bb_re.transpose(0, 2, 1)[:, None], bb_im.transpose(0, 2, 1)[:, None]
    w_re = rp_re[:, :, None, :] * bt_re - rp_im[:, :, None, :] * bt_im
    w_im = rp_re[:, :, None, :] * bt_im + rp_im[:, :, None, :] * bt_re
    w_bd = jnp.concatenate([tile_diag(w_re, 'qgjhp->qjghp'),
                            tile_diag(w_im, 'qgjhp->qjghp')], axis=-1)
    q_re, q_im = pw_re[:, 1:], pw_im[:, 1:]
    v_re = c_re[:, None] * q_re[:, :, None, :] - c_im[:, None] * q_im[:, :, None, :]
    v_im = c_re[:, None] * q_im[:, :, None, :] + c_im[:, None] * q_re[:, :, None, :]
    v_bd = jnp.concatenate([tile_diag(v_re, 'qgihp->qigph'),
                            tile_diag(-v_im, 'qgihp->qigph')], axis=2)
    at_re = pw_re[:, t].reshape(nt, 1, tg * S5_STATE)
    at_im = pw_im[:, t].reshape(nt, 1, tg * S5_STATE)
    w_pair = w_bd.reshape(nt, t // 2, 2 * LANES, w_bd.shape[-1])
    k_all = k_bd.transpose(0, 2, 1, 3).reshape(nt, LANES, t * LANES)
    k_lag = jnp.concatenate([jnp.zeros((nt, LANES, LANES), BF16), k_all[:, :, :-LANES]], axis=2)
    k_pair = jnp.concatenate([k_all, k_lag], axis=1)
    v_all = v_bd.transpose(0, 2, 1, 3).reshape(nt, v_bd.shape[2], t * LANES)
    return w_pair, k_pair, v_all, at_re, at_im


def _outproj_kernel(ym_ref, ys_ref, yw_ref, y5_ref, u_ref, g5_ref, x_ref, w_ref, pg_ref,
                    d5_ref, gw_ref, gb_ref, o_ref):
    y5 = y5_ref[...] + d5_ref[...] * u_ref[...]
    c0 = math.sqrt(2.0 / math.pi)
    y5 = y5 * (0.5 * (1.0 + jnp.tanh(c0 * (y5 + 0.044715 * (y5 * y5 * y5)))))
    gl = jnp.dot(y5.astype(BF16), gw_ref[...], preferred_element_type=F32) + gb_ref[...]
    y5 = y5 * _sigmoid(gl) * _silu(g5_ref[...])
    parts = [ym_ref[...], ys_ref[...], yw_ref[...], y5.astype(BF16)]
    acc = None
    for i, part in enumerate(parts):
        d = jnp.dot(part, w_ref[i * PROJ_TILE:(i + 1) * PROJ_TILE, :], preferred_element_type=F32)
        acc = d if acc is None else acc + d
    ms = jnp.mean(acc * acc, axis=-1, keepdims=True)
    o_ref[...] = x_ref[...] + acc * lax.rsqrt(ms + NORM_EPS) * pg_ref[...]


def _outproj(y_moba, y_ssd, y_swa, y_s5, proj, x2, w_out, post_g, s5_d, glu_w, glu_b, layer, tm):
    m = x2.shape[0]
    rows = lambda w: pl.BlockSpec((tm, w), lambda i: (i, 0))
    full = lambda shape: pl.BlockSpec(shape, lambda i: (0, 0))
    of_layer = lambda shape: pl.BlockSpec((None,) + shape, lambda i: (layer, 0, 0))
    return pl.pallas_call(
        _outproj_kernel,
        grid=(m // tm,),
        in_specs=[
            rows(MOBA_W), rows(SSD_W), rows(SWA_W), rows(S5_W),
            pl.BlockSpec((tm, PROJ_TILE), lambda i: (i, T_SU)),
            pl.BlockSpec((tm, PROJ_TILE), lambda i: (i, T_S5G)),
            rows(D_MODEL),
            of_layer((MIX_W, D_MODEL)),
            full((1, D_MODEL)),
            full((1, S5_W)),
            of_layer((S5_W, S5_W)),
            full((1, S5_W)),
        ],
        out_specs=rows(D_MODEL),
        out_shape=jax.ShapeDtypeStruct((m, D_MODEL), F32),
        compiler_params=pltpu.CompilerParams(
            dimension_semantics=("parallel",),
            vmem_limit_bytes=VMEM_LIMIT_BYTES),
        name="outproj",
    )(y_moba, y_ssd, y_swa, y_s5, proj, proj, x2, w_out, post_g, s5_d, glu_w, glu_b)


def _swa_tiles(w, axis):
    s = w.shape
    per_kv = SWA_HEADS // SWA_KV_HEADS
    w = w.reshape(s[:axis] + (SWA_KV_HEADS, per_kv, HEAD_DIM) + s[axis + 1:])
    return jnp.swapaxes(w, axis, axis + 1).reshape(s)


def _rope_tables(seq):
    inv = 1.0 / (ROPE_THETA ** (jnp.arange(0, HEAD_DIM, 2, dtype=F32) / HEAD_DIM))
    ang = jnp.arange(seq, dtype=F32)[:, None] * inv[None, :]
    cos, sin = jnp.cos(ang), jnp.sin(ang)
    return (jnp.concatenate([cos, cos, cos, cos], axis=1),
            jnp.concatenate([-sin, -sin, sin, sin], axis=1))


(IN_MQ, IN_MK, IN_MV, IN_MG, IN_XBC, IN_DT, IN_Z, IN_SQ, IN_SK, IN_SV, IN_SG, IN_SU, IN_S5G, IN_W) = (
    int(v) for v in np.cumsum([0, MOBA_W, MOBA_W, MOBA_W, MOBA_W, SSD_CONV_CH, SSD_HEADS, SSD_W,
                               SWA_W, SWA_KV_W, SWA_KV_W, SWA_W, S5_W, S5_W]))
WPREP_COLS = 256


def _wprep_moves():
    hx = HEAD_DIM // 2
    moves = []

    def straight(dst, src, n):
        moves.append((dst, src, n))

    def pair_tile(dst, head_a, head_b):
        for q, src in enumerate((head_a, head_b, head_a + hx, head_b + hx)):
            moves.append((dst + q * hx, src, hx))

    per = PROJ_TILE // LANES
    per_kv = SWA_HEADS // SWA_KV_HEADS
    for i in range(per):
        pair_tile(T_MQ * PROJ_TILE + i * LANES, IN_MQ + i * LANES, IN_MQ + i * LANES + HEAD_DIM)
        pair_tile(T_MK * PROJ_TILE + i * LANES, IN_MK + i * LANES, IN_MK + i * LANES + HEAD_DIM)
        pair_tile(T_SQ * PROJ_TILE + i * LANES, IN_SQ + i * HEAD_DIM, IN_SQ + (i + per_kv) * HEAD_DIM)
        straight(T_SG * PROJ_TILE + i * LANES, IN_SG + i * HEAD_DIM, HEAD_DIM)
        straight(T_SG * PROJ_TILE + i * LANES + HEAD_DIM, IN_SG + (i + per_kv) * HEAD_DIM, HEAD_DIM)
    straight(T_MV * PROJ_TILE, IN_MV, MOBA_W)
    straight(T_MG * PROJ_TILE, IN_MG, MOBA_W)
    straight(T_XBC * PROJ_TILE, IN_XBC, SSD_CONV_CH)
    straight(T_Z * PROJ_TILE, IN_Z, SSD_W)
    straight(T_SU * PROJ_TILE, IN_SU, S5_W)
    straight(T_S5G * PROJ_TILE, IN_S5G, S5_W)
    pair_tile(TAIL_K_BLK * LANES, IN_SK, IN_SK + HEAD_DIM)
    straight(TAIL_V_BLK * LANES, IN_SV, SWA_KV_W)
    return moves


def _wprep_kernel(w_ref, o_ref):
    for dst, src, n in _wprep_moves():
        o_ref[dst:dst + n, :] = w_ref[src:src + n, :].astype(BF16)
    dt0 = TAIL_DT_BLK * LANES
    tail_end = (T_TAIL + 1) * PROJ_TILE
    pad_rows = tail_end - dt0 - SSD_HEADS
    o_ref[dt0:tail_end, :] = jnp.concatenate(
        [w_ref[IN_DT:IN_DT + SSD_HEADS, :], jnp.zeros((pad_rows, w_ref.shape[1]), F32)],
        axis=0).astype(BF16)


def _cat_w_in(w_in):
    w_t = jnp.swapaxes(w_in, 1, 2)
    depth, _, d = w_t.shape
    tc = WPREP_COLS
    return pl.pallas_call(
        _wprep_kernel,
        grid=(depth, d // tc),
        in_specs=[pl.BlockSpec((None, IN_W, tc), lambda l, c: (l, 0, c))],
        out_specs=pl.BlockSpec((None, PROJ_W, tc), lambda l, c: (l, 0, c)),
        out_shape=jax.ShapeDtypeStruct((depth, PROJ_W, d), BF16),
        compiler_params=pltpu.CompilerParams(
            dimension_semantics=("parallel", "parallel"),
            vmem_limit_bytes=VMEM_LIMIT_BYTES),
        name="wprep",
    )(w_t)


def _pad_lanes(v):
    return jnp.pad(v, (0, LANES - v.shape[0]))[None, :]


def _layer(layer, x2, b, seq, cos_t, sin_t, pre_g, post_g, w_in, w_out, conv_w, conv_b, dt_bias, a_log,
           ssd_d, ssd_norm, sinks, s5_tabs, s5_d, glu_w, glu_b):
    proj = _inproj(x2, pre_g[None, :], w_in, layer, cos_t, sin_t, seq, min(TM_INPROJ, seq))
    proj3 = proj.reshape(b, seq, PROJ_W)

    y_moba = _moba(proj3)
    y_swa = _swa(proj3, sinks)
    y_ssd = _ssd(proj3, conv_w, conv_b[None, :], _pad_lanes(dt_bias), _pad_lanes(a_log),
                 jnp.repeat(ssd_d, SSD_HEAD_DIM)[None, :], ssd_norm[None, :])
    y_s5 = _s5_core(proj3, *s5_tabs, layer).reshape(b * seq, S5_W)

    m = b * seq
    return _outproj(y_moba.reshape(m, MOBA_W), y_ssd.reshape(m, SSD_W), y_swa.reshape(m, SWA_W),
                    y_s5, proj, x2, w_out, post_g[None, :], s5_d[None, :],
                    glu_w, glu_b[None, :], layer, min(TM_OUTPROJ, seq))


def kernel(x, pre_norm, post_norm, w_in, w_out, ssd_conv_w, ssd_conv_b, ssd_dt_bias, ssd_a_log, ssd_d, ssd_norm, swa_sinks, s5_a_re, s5_a_im, s5_log_dt, s5_b_re, s5_b_im, s5_c_re, s5_c_im, s5_d, s5_glu_w, s5_glu_b):
    b, seq, d = x.shape
    assert d == D_MODEL
    assert seq % MOBA_BLOCK == 0 and seq % SSD_CHUNK == 0 and seq % SWA_WINDOW == 0
    assert seq % (S5_CHUNK * S5_SCAN_ROWS) == 0
    assert seq % SWA_TQ == 0
    depth = pre_norm.shape[0]
    cos_t, sin_t = _rope_tables(seq)
    w_cat = _cat_w_in(w_in)
    swa_lo = MOBA_W + SSD_W
    w_out_p = jnp.concatenate([w_out[:, :swa_lo], _swa_tiles(w_out[:, swa_lo:swa_lo + SWA_W], 1),
                               w_out[:, swa_lo + SWA_W:]], axis=1).astype(BF16)
    per_kv = SWA_HEADS // SWA_KV_HEADS
    sinks_p = jnp.swapaxes(swa_sinks.reshape(depth, SWA_KV_HEADS, per_kv), 1, 2).reshape(depth, SWA_HEADS)
    glu_w16 = s5_glu_w.astype(BF16)
    fold = lambda a: a.reshape((depth * S5_GROUPS,) + a.shape[2:])
    tabs = _s5_tables(fold(s5_a_re), fold(s5_a_im), s5_log_dt.reshape(-1), fold(s5_b_re), fold(s5_b_im),
                      fold(s5_c_re), fold(s5_c_im))

    x2 = x.reshape(b * seq, d)
    for l in range(depth):
        x2 = _layer(l, x2, b, seq, cos_t, sin_t, pre_norm[l], post_norm[l], w_cat, w_out_p,
                    ssd_conv_w[l], ssd_conv_b[l], ssd_dt_bias[l], ssd_a_log[l], ssd_d[l], ssd_norm[l],
                    sinks_p[l], tabs, s5_d[l], glu_w16, s5_glu_b[l])
    return x2.reshape(b, seq, d)
```

```python
import functools
import math

import jax
import jax.numpy as jnp
import numpy as np
from jax import lax
from jax.experimental import pallas as pl
from jax.experimental.pallas import tpu as pltpu

F32 = jnp.float32
BF16 = jnp.bfloat16

D_MODEL = 2048
HEAD_DIM = 64
ROPE_THETA = 10000.0
NORM_EPS = 1e-6
MOBA_HEADS = 8
MOBA_W = MOBA_HEADS * HEAD_DIM
MOBA_BLOCK = 256
MOBA_TOPK = 3
SSD_HEADS = 8
SSD_HEAD_DIM = 64
SSD_W = SSD_HEADS * SSD_HEAD_DIM
SSD_GROUPS = 2
SSD_STATE = 128
SSD_CONV = 4
SSD_CHUNK = 256
SSD_CONV_CH = SSD_W + 2 * SSD_GROUPS * SSD_STATE
SWA_HEADS = 8
SWA_KV_HEADS = 2
SWA_W = SWA_HEADS * HEAD_DIM
SWA_KV_W = SWA_KV_HEADS * HEAD_DIM
SWA_WINDOW = 128
S5_W = 512
S5_GROUP = 16
S5_GROUPS = S5_W // S5_GROUP
S5_STATE = 64
MIX_W = MOBA_W + SSD_W + SWA_W + S5_W

LANES = 128
SUBLANES = 8
VMEM_LIMIT_BYTES = 48 * 1024 * 1024

PROJ_TILE = 512
T_MQ, T_MK, T_MV, T_MG, T_XBC, T_Z, T_SG, T_SQ, T_TAIL, T_SU, T_S5G = 0, 1, 2, 3, 4, 6, 7, 8, 9, 10, 11
PROJ_W = 12 * PROJ_TILE
INPROJ_TN = 2 * PROJ_TILE
assert T_MK == T_MQ + 1 and T_MQ % 2 == 0 and T_TAIL == T_SQ + 1 and T_SQ % 2 == 0
TAIL_K_BLK = T_TAIL * (PROJ_TILE // LANES)
TAIL_V_BLK = TAIL_K_BLK + 1
TAIL_DT_BLK = TAIL_K_BLK + 2

NEG = -1e30
TM_INPROJ = 1024
TM_OUTPROJ = 512
SWA_TQ = 512
MOBA_PAIRS = 4
MOBA_ONES_ROWS = 16
S5_CHUNK = 8
S5_TILE_GROUPS = LANES // S5_GROUP
S5_SCAN_ROWS = SUBLANES


def _silu(x):
    return x / (1.0 + jnp.exp(-x))


def _sigmoid(x):
    return 1.0 / (1.0 + jnp.exp(-x))


def _bf16_parts(x):
    hi = x.astype(BF16)
    r1 = x - hi.astype(F32)
    mid = r1.astype(BF16)
    lo = (r1 - mid.astype(F32)).astype(BF16)
    return hi, mid, lo


def _nt_dot(a, b, precision=None):
    return lax.dot_general(a, b, (((1,), (1,)), ((), ())), precision=precision,
                           preferred_element_type=F32)


def _tn_dot(a, b, precision=None):
    return lax.dot_general(a, b, (((0,), (0,)), ((), ())), precision=precision,
                           preferred_element_type=F32)


def _inproj_kernel(x_ref, g_ref, w_ref, cos_ref, sin_ref, o_ref, h_scr):
    j = pl.program_id(1)

    @pl.when(j == 0)
    def _():
        x = x_ref[...]
        ms = jnp.mean(x * x, axis=-1, keepdims=True)
        h_scr[...] = (x * lax.rsqrt(ms + NORM_EPS) * g_ref[...]).astype(BF16)

    acc = _nt_dot(h_scr[...], w_ref[...])
    j_qk = T_MQ * PROJ_TILE // INPROJ_TN
    j_swa = T_SQ * PROJ_TILE // INPROJ_TN

    def store(n_rope):
        cos, sin = cos_ref[...], sin_ref[...]
        for t in range(acc.shape[1] // LANES):
            a = acc[:, t * LANES:(t + 1) * LANES]
            if t < n_rope:
                a = a * cos + pltpu.roll(a, LANES // 2, axis=1) * sin
            o_ref[:, t * LANES:(t + 1) * LANES] = a

    @pl.when(j == j_qk)
    def _():
        store(INPROJ_TN // LANES)

    @pl.when(j == j_swa)
    def _():
        store((SWA_W + SWA_KV_W) // LANES)

    @pl.when((j != j_qk) & (j != j_swa))
    def _():
        o_ref[...] = acc


def _inproj(x2, pre_g, w_cat, layer, cos_t, sin_t, seq, tm):
    m = x2.shape[0]
    nseq = seq // tm
    return pl.pallas_call(
        _inproj_kernel,
        grid=(m // tm, PROJ_W // INPROJ_TN),
        in_specs=[
            pl.BlockSpec((tm, D_MODEL), lambda i, j: (i, 0)),
            pl.BlockSpec((1, D_MODEL), lambda i, j: (0, 0)),
            pl.BlockSpec((None, INPROJ_TN, D_MODEL), lambda i, j: (layer, j, 0)),
            pl.BlockSpec((tm, LANES), lambda i, j: (i % nseq, 0)),
            pl.BlockSpec((tm, LANES), lambda i, j: (i % nseq, 0)),
        ],
        out_specs=pl.BlockSpec((tm, INPROJ_TN), lambda i, j: (i, j)),
        out_shape=jax.ShapeDtypeStruct((m, PROJ_W), F32),
        scratch_shapes=[pltpu.VMEM((tm, D_MODEL), BF16)],
        compiler_params=pltpu.CompilerParams(
            dimension_semantics=("parallel", "arbitrary"),
            vmem_limit_bytes=VMEM_LIMIT_BYTES),
        name="inproj",
    )(x2, pre_g, w_cat, cos_t, sin_t)


def _pair_head(idx):
    return (idx // (HEAD_DIM // 2)) % 2


def _moba_kernel(q_ref, k_ref, v_ref, g_ref, o_ref, kaug_scr, vt_scr, kmean_scr, qaug_scr,
                 acc_scr, sa_scr, sb_scr, *, nb):
    qi = pl.program_id(2)
    blk = MOBA_BLOCK
    qscale = (HEAD_DIM ** -0.5) * math.log2(math.e)

    nh = 2 * MOBA_PAIRS

    @pl.when(qi == 0)
    def _():
        seq = k_ref.shape[0]
        row_blk = lax.broadcasted_iota(jnp.int32, (seq, LANES), 0) // blk
        lane = lax.broadcasted_iota(jnp.int32, (seq, LANES), 1)
        onehot = jnp.where(row_blk == lane, 1.0, 0.0).astype(BF16)
        kmean_scr[...] = jnp.mean(k_ref[...].reshape(nb, blk, MOBA_PAIRS * LANES), axis=1)
        for pp in range(MOBA_PAIRS):
            kaug_scr[pp, :, :LANES] = k_ref[:, pp * LANES:(pp + 1) * LANES].astype(BF16)
            kaug_scr[pp, :, LANES:] = onehot
        ones = jnp.ones((MOBA_ONES_ROWS, blk), BF16)
        for n in range(nb):
            vt = v_ref[n * blk:(n + 1) * blk, :].T.astype(BF16)
            for hd in range(nh):
                vt_scr[hd, n, 0:HEAD_DIM, :] = vt[hd * HEAD_DIM:(hd + 1) * HEAD_DIM, :]
                vt_scr[hd, n, HEAD_DIM:, :] = ones

    q_all = q_ref[...].T
    sub = lax.broadcasted_iota(jnp.int32, (LANES, blk), 0)
    klane = lax.broadcasted_iota(jnp.int32, (nb, LANES), 1)
    blk_id = lax.broadcasted_iota(jnp.int32, (nb, blk), 0)
    past = blk_id < qi

    for hd in range(nh):
        pp, hh = divmod(hd, 2)
        q_t = q_all[pp * LANES:(pp + 1) * LANES]
        km = jnp.where(_pair_head(klane) == hh, kmean_scr[:, pp * LANES:(pp + 1) * LANES], 0.0)
        gate = jnp.dot(km, q_t, precision=lax.Precision.HIGHEST, preferred_element_type=F32)
        gm = jnp.where(past, gate, -jnp.inf)
        cnt = jnp.zeros((nb, blk), F32)
        for n in range(nb):
            gn = gm[n:n + 1, :]
            beats = (gn > gm) | ((gn == gm) & (blk_id > n))
            cnt = cnt + jnp.where(beats & (qi > n), 1.0, 0.0)
        keep = (past & (cnt < float(MOBA_TOPK))) | (blk_id == qi)
        pen = jnp.where(keep, 0.0, NEG)
        pen = jnp.concatenate([pen, jnp.zeros((LANES - nb, blk), F32)], axis=0)
        qh = jnp.where(_pair_head(sub) == hh, q_t, 0.0) * qscale
        qaug_scr[hd] = jnp.concatenate([qh.astype(BF16), pen.astype(BF16)], axis=0)
        acc_scr[hd] = jnp.zeros(acc_scr.shape[1:], F32)

    def qk(t, s_ref):
        start = pl.multiple_of(t * blk, blk)
        for pp in range(MOBA_PAIRS):
            kb = kaug_scr[pp, pl.ds(start, blk), :]
            for hd in (2 * pp, 2 * pp + 1):
                s_ref[hd] = jnp.dot(kb, qaug_scr[hd], preferred_element_type=F32)

    def process(t, s_ref, ms, tail):
        out = []
        for hd in range(nh):
            s = s_ref[hd]
            if tail:
                s = jnp.where(key_minus_qry <= (qi - t) * blk, s, NEG)
            m_new = jnp.maximum(ms[hd], jnp.max(s, axis=0, keepdims=True))
            alpha = jnp.exp2(ms[hd] - m_new)
            p = jnp.exp2(s - m_new).astype(BF16)
            pv = jnp.dot(vt_scr[hd, jnp.minimum(t, qi)], p, preferred_element_type=F32)
            acc_scr[hd] = alpha * acc_scr[hd] + pv
            out.append(m_new)
        return tuple(out)

    qk(0, sa_scr)

    def body(k, ms):
        t = 2 * k
        qk(t + 1, sb_scr)
        ms = process(t, sa_scr, ms, False)
        qk(t + 2, sa_scr)
        return process(t + 1, sb_scr, ms, False)

    m_init = jnp.full((1, blk), NEG, F32)
    ms = lax.fori_loop(0, qi // 2, body, (m_init,) * nh)
    key_minus_qry = (lax.broadcasted_iota(jnp.int32, (blk, blk), 0)
                     - lax.broadcasted_iota(jnp.int32, (blk, blk), 1))
    t1 = 2 * (qi // 2)
    has_second = t1 < qi

    @pl.when(has_second)
    def _():
        qk(t1 + 1, sb_scr)
        process(t1 + 1, sb_scr, process(t1, sa_scr, ms, False), True)

    @pl.when(jnp.logical_not(has_second))
    def _():
        process(t1, sa_scr, ms, True)

    outs = [acc_scr[hd][:HEAD_DIM] / acc_scr[hd][HEAD_DIM:HEAD_DIM + 1] for hd in range(nh)]
    out = jnp.concatenate(outs, axis=0).T
    o_ref[...] = (out * _silu(g_ref[...])).astype(o_ref.dtype)


def _moba(proj3):
    b, seq, _ = proj3.shape
    blk = MOBA_BLOCK
    nb = seq // blk
    wid = MOBA_PAIRS * LANES
    ngrp = MOBA_W // wid
    per = PROJ_TILE // wid
    nh = 2 * MOBA_PAIRS
    return pl.pallas_call(
        functools.partial(_moba_kernel, nb=nb),
        grid=(b, ngrp, nb),
        in_specs=[
            pl.BlockSpec((None, blk, wid), lambda bi, hp, qi: (bi, qi, T_MQ * per + hp)),
            pl.BlockSpec((None, seq, wid), lambda bi, hp, qi: (bi, 0, T_MK * per + hp),
                         pipeline_mode=pl.Buffered(1)),
            pl.BlockSpec((None, seq, wid), lambda bi, hp, qi: (bi, 0, T_MV * per + hp),
                         pipeline_mode=pl.Buffered(1)),
            pl.BlockSpec((None, blk, wid), lambda bi, hp, qi: (bi, qi, T_MG * per + hp)),
        ],
        out_specs=pl.BlockSpec((None, blk, wid), lambda bi, hp, qi: (bi, qi, hp)),
        out_shape=jax.ShapeDtypeStruct((b, seq, MOBA_W), BF16),
        scratch_shapes=[pltpu.VMEM((MOBA_PAIRS, seq, 2 * LANES), BF16),
                        pltpu.VMEM((nh, nb, HEAD_DIM + MOBA_ONES_ROWS, blk), BF16),
                        pltpu.VMEM((nb, wid), F32),
                        pltpu.VMEM((nh, 2 * LANES, blk), BF16),
                        pltpu.VMEM((nh, HEAD_DIM + MOBA_ONES_ROWS, blk), F32),
                        pltpu.VMEM((nh, blk, blk), F32),
                        pltpu.VMEM((nh, blk, blk), F32)],
        compiler_params=pltpu.CompilerParams(
            dimension_semantics=("parallel", "parallel", "arbitrary"),
            vmem_limit_bytes=VMEM_LIMIT_BYTES),
        name="moba",
    )(proj3, proj3, proj3, proj3)


def _swa_kernel(sink_ref, q_ref, kp_ref, kc_ref, vp_ref, vc_ref, g_ref, o_ref):
    n = pl.program_id(1)
    w = SWA_WINDOW
    scale = HEAD_DIM ** -0.5
    kcat = jnp.concatenate([kp_ref[...], kc_ref[...]], axis=0).astype(BF16)
    vcat = jnp.concatenate([vp_ref[...], vc_ref[...]], axis=0).astype(BF16)
    row = lax.broadcasted_iota(jnp.int32, (w, 2 * w), 0)
    col = lax.broadcasted_iota(jnp.int32, (w, 2 * w), 1)
    band = (col > row) & (col <= row + w)
    lane = lax.broadcasted_iota(jnp.int32, (w, LANES), 1)
    for win in range(SWA_TQ // w):
        valid = band & ((n > 0) | (col >= w)) if win == 0 else band
        kw = kcat[win * w:(win + 2) * w]
        vw = vcat[win * w:(win + 2) * w]
        rows = slice(win * w, (win + 1) * w)
        for i in range(SWA_W // LANES):
            q_t = q_ref[rows, i * LANES:(i + 1) * LANES]
            halves = []
            for c in range(2):
                qm = (jnp.where(_pair_head(lane) == c, q_t, 0.0) * scale).astype(BF16)
                s = _nt_dot(qm, kw)
                s = jnp.where(valid, s, NEG)
                sink = sink_ref[2 * i + c]
                m = jnp.maximum(jnp.max(s, axis=-1, keepdims=True), sink)
                e = jnp.exp(s - m)
                denom = jnp.sum(e, axis=-1, keepdims=True) + jnp.exp(sink - m)
                o = jnp.dot(e.astype(BF16), vw, preferred_element_type=F32)
                halves.append(o / denom)
            out = jnp.where(lane < HEAD_DIM, halves[0], halves[1])
            g_t = g_ref[rows, i * LANES:(i + 1) * LANES]
            o_ref[rows, i * LANES:(i + 1) * LANES] = (out * _silu(g_t)).astype(o_ref.dtype)


def _swa(proj3, sinks_perm):
    b, seq, _ = proj3.shape
    w = SWA_WINDOW
    tq = SWA_TQ
    per = tq // w
    prev = lambda bi, n: (bi, jnp.maximum(n * per - 1, 0), TAIL_K_BLK)
    prev_v = lambda bi, n: (bi, jnp.maximum(n * per - 1, 0), TAIL_V_BLK)
    return pl.pallas_call(
        _swa_kernel,
        grid=(b, seq // tq),
        in_specs=[
            pl.BlockSpec(memory_space=pltpu.SMEM),
            pl.BlockSpec((None, tq, SWA_W), lambda bi, n: (bi, n, T_SQ)),
            pl.BlockSpec((None, w, LANES), prev),
            pl.BlockSpec((None, tq, LANES), lambda bi, n: (bi, n, TAIL_K_BLK)),
            pl.BlockSpec((None, w, LANES), prev_v),
            pl.BlockSpec((None, tq, LANES), lambda bi, n: (bi, n, TAIL_V_BLK)),
            pl.BlockSpec((None, tq, SWA_W), lambda bi, n: (bi, n, T_SG)),
        ],
        out_specs=pl.BlockSpec((None, tq, SWA_W), lambda bi, n: (bi, n, 0)),
        out_shape=jax.ShapeDtypeStruct((b, seq, SWA_W), BF16),
        compiler_params=pltpu.CompilerParams(
            dimension_semantics=("parallel", "arbitrary"),
            vmem_limit_bytes=VMEM_LIMIT_BYTES),
        name="swa",
    )(sinks_perm, proj3, proj3, proj3, proj3, proj3, proj3)


def _ssd_kernel(xbc_ref, z_ref, dt_ref, cw_ref, cb_ref, dtb_ref, alog_ref, dsk_ref, nw_ref,
                o_ref, xbuf, state):
    c = pl.program_id(1)
    s = SSD_CHUNK
    pad = SUBLANES
    hi = lax.Precision.HIGHEST

    @pl.when(c == 0)
    def _():
        xbuf[0:pad, :] = jnp.zeros((pad, SSD_CONV_CH), F32)
        state[...] = jnp.zeros_like(state)

    @pl.when(c > 0)
    def _():
        xbuf[0:pad, :] = xbuf[s:s + pad, :]

    xbuf[pad:pad + s, :] = xbc_ref[...]
    conv = cb_ref[...]
    for k in range(SSD_CONV):
        off = pad - (SSD_CONV - 1) + k
        conv = conv + cw_ref[k:k + 1, :] * xbuf[off:off + s, :]
    xbc = _silu(conv)
    xs = xbc[:, :SSD_W]
    gw = SSD_GROUPS * SSD_STATE
    bm = xbc[:, SSD_W:SSD_W + gw]
    cm = xbc[:, SSD_W + gw:]

    dtr = dt_ref[...] + dtb_ref[...]
    dt = jnp.maximum(dtr, 0.0) + jnp.log(1.0 + jnp.exp(-jnp.abs(dtr)))
    a = dt * (-jnp.exp(alog_ref[...]))
    row = lax.broadcasted_iota(jnp.int32, (s, s), 0)
    col = lax.broadcasted_iota(jnp.int32, (s, s), 1)
    lower = row >= col
    tri = jnp.where(lower, 1.0, 0.0).astype(BF16)
    acum = sum(jnp.dot(tri, part, preferred_element_type=F32) for part in _bf16_parts(a))
    acum_t = acum.T
    alast = acum[s - 1:s, :]

    erow = lax.broadcasted_iota(jnp.int32, (LANES, SSD_W), 0)
    ecol = lax.broadcasted_iota(jnp.int32, (LANES, SSD_W), 1)
    expand = jnp.where(erow == ecol // SSD_HEAD_DIM, 1.0, 0.0).astype(BF16)
    cols = jnp.concatenate([dt, jnp.exp(alast - acum), jnp.exp(acum),
                            jnp.broadcast_to(jnp.exp(alast), (SUBLANES, LANES))], axis=0)
    cols_e = sum(jnp.dot(part, expand, preferred_element_type=F32) for part in _bf16_parts(cols))
    dt_e, dec_e, ea_e = cols_e[:s], cols_e[s:2 * s], cols_e[2 * s:3 * s]
    cd_e = cols_e[3 * s:3 * s + 1]

    x_dt = xs * dt_e
    x_dec = (x_dt * dec_e).astype(BF16)
    x_dt16 = x_dt.astype(BF16)
    lane = lax.broadcasted_iota(jnp.int32, (s, LANES), 1)
    hpg = SSD_HEADS // SSD_GROUPS
    gwid = hpg * SSD_HEAD_DIM
    y_parts = []
    for g in range(SSD_GROUPS):
        bg = bm[:, g * SSD_STATE:(g + 1) * SSD_STATE].astype(BF16)
        cg = cm[:, g * SSD_STATE:(g + 1) * SSD_STATE].astype(BF16)
        cbm = _nt_dot(cg, bg)
        st = state[:, g * gwid:(g + 1) * gwid]
        y_off = jnp.dot(cg, st.astype(BF16), preferred_element_type=F32) \
            * ea_e[:, g * gwid:(g + 1) * gwid]
        new_st = cd_e[:, g * gwid:(g + 1) * gwid] * st + _tn_dot(bg, x_dec[:, g * gwid:(g + 1) * gwid])
        state[:, g * gwid:(g + 1) * gwid] = new_st
        for i in range(hpg // 2):
            lo = g * gwid + i * LANES
            xp = x_dt16[:, lo:lo + LANES]
            ys = []
            for hh in range(2):
                h = g * hpg + 2 * i + hh
                seg = acum[:, h:h + 1] - acum_t[h:h + 1, :]
                lm = jnp.where(lower, jnp.exp(jnp.minimum(seg, 0.0)), 0.0)
                ys.append(jnp.dot((cbm * lm).astype(BF16), xp, preferred_element_type=F32))
            y_parts.append(jnp.where(lane < SSD_HEAD_DIM, ys[0], ys[1])
                           + y_off[:, i * LANES:(i + 1) * LANES])
    y = jnp.concatenate(y_parts, axis=1)
    y = (y + dsk_ref[...] * xs) * _silu(z_ref[...])
    outs = []
    for g in range(SSD_GROUPS):
        yg = y[:, g * gwid:(g + 1) * gwid]
        outs.append(yg * lax.rsqrt(jnp.mean(yg * yg, axis=-1, keepdims=True) + NORM_EPS))
    o_ref[...] = (jnp.concatenate(outs, axis=1) * nw_ref[...]).astype(o_ref.dtype)


def _ssd(proj3, conv_w, conv_b, dt_bias_p, a_log_p, d_skip_e, norm_w):
    b, seq, _ = proj3.shape
    s = SSD_CHUNK
    full = lambda shape: pl.BlockSpec(shape, lambda bi, c: (0, 0))
    return pl.pallas_call(
        _ssd_kernel,
        grid=(b, seq // s),
        in_specs=[
            pl.BlockSpec((None, s, SSD_CONV_CH), lambda bi, c: (bi, c, T_XBC * PROJ_TILE // SSD_CONV_CH)),
            pl.BlockSpec((None, s, SSD_W), lambda bi, c: (bi, c, T_Z)),
            pl.BlockSpec((None, s, LANES), lambda bi, c: (bi, c, TAIL_DT_BLK)),
            full((SSD_CONV, SSD_CONV_CH)),
            full((1, SSD_CONV_CH)),
            full((1, LANES)),
            full((1, LANES)),
            full((1, SSD_W)),
            full((1, SSD_W)),
        ],
        out_specs=pl.BlockSpec((None, s, SSD_W), lambda bi, c: (bi, c, 0)),
        out_shape=jax.ShapeDtypeStruct((b, seq, SSD_W), BF16),
        scratch_shapes=[pltpu.VMEM((s + 2 * SUBLANES, SSD_CONV_CH), F32),
                        pltpu.VMEM((SSD_STATE, SSD_W), F32)],
        compiler_params=pltpu.CompilerParams(
            dimension_semantics=("parallel", "arbitrary"),
            vmem_limit_bytes=VMEM_LIMIT_BYTES),
        name="ssd",
    )(proj3, proj3, proj3, conv_w, conv_b, dt_bias_p, a_log_p, d_skip_e, norm_w)


def _s5_kernel(u_ref, w_ref, k_ref, v_ref, are_ref, aim_ref, y_ref, x_scr, e_scr, s_scr, y_scr,
               *, nrow):
    t = S5_CHUNK
    half = S5_TILE_GROUPS * S5_STATE
    for j in range(t):
        x_scr[j // 2, :, (j % 2) * LANES:(j % 2 + 1) * LANES] = (
            u_ref[pl.ds(j, nrow, stride=t), :].astype(BF16))
    e = jnp.dot(x_scr[0], w_ref[0], preferred_element_type=F32)
    for jp in range(1, t // 2):
        e = e + jnp.dot(x_scr[jp], w_ref[jp], preferred_element_type=F32)
    e_scr[...] = e
    a_re = are_ref[...]
    a_im = aim_ref[...]

    def body(it, carry):
        s_re, s_im = carry
        r0 = pl.multiple_of(it * S5_SCAN_ROWS, S5_SCAN_ROWS)
        for kk in range(S5_SCAN_ROWS):
            row = pl.ds(r0 + kk, 1)
            s_scr[row, :half] = s_re
            s_scr[row, half:] = s_im
            e_re, e_im = e_scr[row, :half], e_scr[row, half:]
            s_re, s_im = a_re * s_re - a_im * s_im + e_re, a_re * s_im + a_im * s_re + e_im
        return s_re, s_im

    zero = jnp.zeros((1, half), F32)
    lax.fori_loop(0, nrow // S5_SCAN_ROWS, body, (zero, zero))
    y_scr[...] = jnp.dot(s_scr[...].astype(BF16), v_ref[...], preferred_element_type=F32)
    for jp in range(t // 2):
        j = 2 * jp
        y_scr[:, j * LANES:] += jnp.dot(x_scr[jp], k_ref[:, :(t - j) * LANES],
                                        preferred_element_type=F32)
    for i in range(t):
        y_ref[pl.ds(i, nrow, stride=t), :] = y_scr[:, i * LANES:(i + 1) * LANES]


def _s5_core(proj3, w, k, v, a_re, a_im, layer):
    b, seq, _ = proj3.shape
    t = S5_CHUNK
    nrow = seq // t
    ntile = S5_W // LANES
    half = S5_TILE_GROUPS * S5_STATE
    per = PROJ_TILE // LANES
    base = layer * ntile
    return pl.pallas_call(
        functools.partial(_s5_kernel, nrow=nrow),
        grid=(ntile, b),
        in_specs=[
            pl.BlockSpec((None, seq, LANES), lambda q, bi: (bi, 0, T_SU * per + q)),
            pl.BlockSpec((None, t // 2, 2 * LANES, 2 * half), lambda q, bi: (base + q, 0, 0, 0)),
            pl.BlockSpec((None, 2 * LANES, t * LANES), lambda q, bi: (base + q, 0, 0)),
            pl.BlockSpec((None, 2 * half, t * LANES), lambda q, bi: (base + q, 0, 0)),
            pl.BlockSpec((None, 1, half), lambda q, bi: (base + q, 0, 0)),
            pl.BlockSpec((None, 1, half), lambda q, bi: (base + q, 0, 0)),
        ],
        out_specs=pl.BlockSpec((None, seq, LANES), lambda q, bi: (bi, 0, q)),
        out_shape=jax.ShapeDtypeStruct((b, seq, S5_W), F32),
        scratch_shapes=[pltpu.VMEM((t // 2, nrow, 2 * LANES), BF16),
                        pltpu.VMEM((nrow, 2 * half), F32),
                        pltpu.VMEM((nrow, 2 * half), F32),
                        pltpu.VMEM((nrow, t * LANES), F32)],
        compiler_params=pltpu.CompilerParams(
            dimension_semantics=("parallel", "parallel"),
            vmem_limit_bytes=VMEM_LIMIT_BYTES),
        name="s5",
    )(proj3, w, k, v, a_re, a_im)


def _s5_tables(a_re, a_im, log_dt, b_re, b_im, c_re, c_im):
    t = S5_CHUNK
    hi = lax.Precision.HIGHEST
    step = jnp.exp(log_dt)[:, None]
    k = jnp.arange(t + 1, dtype=F32)[None, :, None]
    mag = jnp.exp(k * (a_re * step)[:, None, :])
    ang = k * (a_im * step)[:, None, :]
    pw_re, pw_im = mag * jnp.cos(ang), mag * jnp.sin(ang)
    ab_re, ab_im = pw_re[:, 1], pw_im[:, 1]
    den = a_re * a_re + a_im * a_im
    n_re, n_im = ab_re - 1.0, ab_im
    cf_re = (n_re * a_re + n_im * a_im) / den
    cf_im = (n_im * a_re - n_re * a_im) / den
    bb_re = cf_re[..., None] * b_re - cf_im[..., None] * b_im
    bb_im = cf_re[..., None] * b_im + cf_im[..., None] * b_re
    cp_re = c_re[:, None] * pw_re[:, :t, None, :] - c_im[:, None] * pw_im[:, :t, None, :]
    cp_im = c_re[:, None] * pw_im[:, :t, None, :] + c_im[:, None] * pw_re[:, :t, None, :]
    kern = (jnp.einsum('gshp,gpk->gshk', cp_re, bb_re, precision=hi)
            - jnp.einsum('gshp,gpk->gshk', cp_im, bb_im, precision=hi))
    g = a_re.shape[0]
    tg = S5_TILE_GROUPS
    nt = g // tg
    def tile_diag(m, pattern):
        m5 = jnp.einsum(pattern, m.reshape(nt, tg, t, m.shape[2], m.shape[3]).astype(BF16))
        r, c = m5.shape[3], m5.shape[4]
        rep = (jnp.arange(c)[:, None] == jnp.arange(tg * c)[None, :] % c).astype(BF16)
        out = jnp.einsum('qsrc,cn->qsrn', m5.reshape(nt, t, tg * r, c), rep,
                         preferred_element_type=BF16)
        on_diag = (jnp.arange(tg * r)[:, None] // r) == (jnp.arange(tg * c)[None, :] // c)
        return jnp.where(on_diag, out, jnp.zeros((), BF16))

    k_bd = tile_diag(kern, 'qgsab->qsgba')
    k_rev = (t - 1) - jnp.arange(t, dtype=F32)[None, :, None]
    mag_r = jnp.exp(k_rev * (a_re * step)[:, None, :])
    ang_r = k_rev * (a_im * step)[:, None, :]
    rp_re, rp_im = mag_r * jnp.cos(ang_r), mag_r * jnp.sin(ang_r)
    bt_re, bt_im = bb_re.transpose(0, 2, 1)[:, None], bb_im.transpose(0, 2, 1)[:, None]
    w_re = rp_re[:, :, None, :] * bt_re - rp_im[:, :, None, :] * bt_im
    w_im = rp_re[:, :, None, :] * bt_im + rp_im[:, :, None, :] * bt_re
    w_bd = jnp.concatenate([tile_diag(w_re, 'qgjhp->qjghp'),
                            tile_diag(w_im, 'qgjhp->qjghp')], axis=-1)
    q_re, q_im = pw_re[:, 1:], pw_im[:, 1:]
    v_re = c_re[:, None] * q_re[:, :, None, :] - c_im[:, None] * q_im[:, :, None, :]
    v_im = c_re[:, None] * q_im[:, :, None, :] + c_im[:, None] * q_re[:, :, None, :]
    v_bd = jnp.concatenate([tile_diag(v_re, 'qgihp->qigph'),
                            tile_diag(-v_im, 'qgihp->qigph')], axis=2)
    at_re = pw_re[:, t].reshape(nt, 1, tg * S5_STATE)
    at_im = pw_im[:, t].reshape(nt, 1, tg * S5_STATE)
    w_pair = w_bd.reshape(nt, t // 2, 2 * LANES, w_bd.shape[-1])
    k_all = k_bd.transpose(0, 2, 1, 3).reshape(nt, LANES, t * LANES)
    k_lag = jnp.concatenate([jnp.zeros((nt, LANES, LANES), BF16), k_all[:, :, :-LANES]], axis=2)
    k_pair = jnp.concatenate([k_all, k_lag], axis=1)
    v_all = v_bd.transpose(0, 2, 1, 3).reshape(nt, v_bd.shape[2], t * LANES)
    return w_pair, k_pair, v_all, at_re, at_im


def _outproj_kernel(ym_ref, ys_ref, yw_ref, y5_ref, u_ref, g5_ref, x_ref, w_ref, pg_ref,
                    d5_ref, gw_ref, gb_ref, o_ref):
    y5 = y5_ref[...] + d5_ref[...] * u_ref[...]
    c0 = math.sqrt(2.0 / math.pi)
    y5 = y5 * (0.5 * (1.0 + jnp.tanh(c0 * (y5 + 0.044715 * (y5 * y5 * y5)))))
    gl = jnp.dot(y5.astype(BF16), gw_ref[...], preferred_element_type=F32) + gb_ref[...]
    y5 = y5 * _sigmoid(gl) * _silu(g5_ref[...])
    parts = [ym_ref[...], ys_ref[...], yw_ref[...], y5.astype(BF16)]
    acc = None
    for i, part in enumerate(parts):
        d = jnp.dot(part, w_ref[i * PROJ_TILE:(i + 1) * PROJ_TILE, :], preferred_element_type=F32)
        acc = d if acc is None else acc + d
    ms = jnp.mean(acc * acc, axis=-1, keepdims=True)
    o_ref[...] = x_ref[...] + acc * lax.rsqrt(ms + NORM_EPS) * pg_ref[...]


def _outproj(y_moba, y_ssd, y_swa, y_s5, proj, x2, w_out, post_g, s5_d, glu_w, glu_b, layer, tm):
    m = x2.shape[0]
    rows = lambda w: pl.BlockSpec((tm, w), lambda i: (i, 0))
    full = lambda shape: pl.BlockSpec(shape, lambda i: (0, 0))
    of_layer = lambda shape: pl.BlockSpec((None,) + shape, lambda i: (layer, 0, 0))
    return pl.pallas_call(
        _outproj_kernel,
        grid=(m // tm,),
        in_specs=[
            rows(MOBA_W), rows(SSD_W), rows(SWA_W), rows(S5_W),
            pl.BlockSpec((tm, PROJ_TILE), lambda i: (i, T_SU)),
            pl.BlockSpec((tm, PROJ_TILE), lambda i: (i, T_S5G)),
            rows(D_MODEL),
            of_layer((MIX_W, D_MODEL)),
            full((1, D_MODEL)),
            full((1, S5_W)),
            of_layer((S5_W, S5_W)),
            full((1, S5_W)),
        ],
        out_specs=rows(D_MODEL),
        out_shape=jax.ShapeDtypeStruct((m, D_MODEL), F32),
        compiler_params=pltpu.CompilerParams(
            dimension_semantics=("parallel",),
            vmem_limit_bytes=VMEM_LIMIT_BYTES),
        name="outproj",
    )(y_moba, y_ssd, y_swa, y_s5, proj, proj, x2, w_out, post_g, s5_d, glu_w, glu_b)


def _swa_tiles(w, axis):
    s = w.shape
    per_kv = SWA_HEADS // SWA_KV_HEADS
    w = w.reshape(s[:axis] + (SWA_KV_HEADS, per_kv, HEAD_DIM) + s[axis + 1:])
    return jnp.swapaxes(w, axis, axis + 1).reshape(s)


def _rope_tables(seq):
    inv = 1.0 / (ROPE_THETA ** (jnp.arange(0, HEAD_DIM, 2, dtype=F32) / HEAD_DIM))
    ang = jnp.arange(seq, dtype=F32)[:, None] * inv[None, :]
    cos, sin = jnp.cos(ang), jnp.sin(ang)
    return (jnp.concatenate([cos, cos, cos, cos], axis=1),
            jnp.concatenate([-sin, -sin, sin, sin], axis=1))


(IN_MQ, IN_MK, IN_MV, IN_MG, IN_XBC, IN_DT, IN_Z, IN_SQ, IN_SK, IN_SV, IN_SG, IN_SU, IN_S5G, IN_W) = (
    int(v) for v in np.cumsum([0, MOBA_W, MOBA_W, MOBA_W, MOBA_W, SSD_CONV_CH, SSD_HEADS, SSD_W,
                               SWA_W, SWA_KV_W, SWA_KV_W, SWA_W, S5_W, S5_W]))
WPREP_COLS = 256


def _wprep_moves():
    hx = HEAD_DIM // 2
    moves = []

    def straight(dst, src, n):
        moves.append((dst, src, n))

    def pair_tile(dst, head_a, head_b):
        for q, src in enumerate((head_a, head_b, head_a + hx, head_b + hx)):
            moves.append((dst + q * hx, src, hx))

    per = PROJ_TILE // LANES
    per_kv = SWA_HEADS // SWA_KV_HEADS
    for i in range(per):
        pair_tile(T_MQ * PROJ_TILE + i * LANES, IN_MQ + i * LANES, IN_MQ + i * LANES + HEAD_DIM)
        pair_tile(T_MK * PROJ_TILE + i * LANES, IN_MK + i * LANES, IN_MK + i * LANES + HEAD_DIM)
        pair_tile(T_SQ * PROJ_TILE + i * LANES, IN_SQ + i * HEAD_DIM, IN_SQ + (i + per_kv) * HEAD_DIM)
        straight(T_SG * PROJ_TILE + i * LANES, IN_SG + i * HEAD_DIM, HEAD_DIM)
        straight(T_SG * PROJ_TILE + i * LANES + HEAD_DIM, IN_SG + (i + per_kv) * HEAD_DIM, HEAD_DIM)
    straight(T_MV * PROJ_TILE, IN_MV, MOBA_W)
    straight(T_MG * PROJ_TILE, IN_MG, MOBA_W)
    straight(T_XBC * PROJ_TILE, IN_XBC, SSD_CONV_CH)
    straight(T_Z * PROJ_TILE, IN_Z, SSD_W)
    straight(T_SU * PROJ_TILE, IN_SU, S5_W)
    straight(T_S5G * PROJ_TILE, IN_S5G, S5_W)
    pair_tile(TAIL_K_BLK * LANES, IN_SK, IN_SK + HEAD_DIM)
    straight(TAIL_V_BLK * LANES, IN_SV, SWA_KV_W)
    return moves


def _wprep_kernel(w_ref, o_ref):
    for dst, src, n in _wprep_moves():
        o_ref[dst:dst + n, :] = w_ref[src:src + n, :].astype(BF16)
    dt0 = TAIL_DT_BLK * LANES
    tail_end = (T_TAIL + 1) * PROJ_TILE
    pad_rows = tail_end - dt0 - SSD_HEADS
    o_ref[dt0:tail_end, :] = jnp.concatenate(
        [w_ref[IN_DT:IN_DT + SSD_HEADS, :], jnp.zeros((pad_rows, w_ref.shape[1]), F32)],
        axis=0).astype(BF16)


def _cat_w_in(w_in):
    w_t = jnp.swapaxes(w_in, 1, 2)
    depth, _, d = w_t.shape
    tc = WPREP_COLS
    return pl.pallas_call(
        _wprep_kernel,
        grid=(depth, d // tc),
        in_specs=[pl.BlockSpec((None, IN_W, tc), lambda l, c: (l, 0, c))],
        out_specs=pl.BlockSpec((None, PROJ_W, tc), lambda l, c: (l, 0, c)),
        out_shape=jax.ShapeDtypeStruct((depth, PROJ_W, d), BF16),
        compiler_params=pltpu.CompilerParams(
            dimension_semantics=("parallel", "parallel"),
            vmem_limit_bytes=VMEM_LIMIT_BYTES),
        name="wprep",
    )(w_t)


def _pad_lanes(v):
    return jnp.pad(v, (0, LANES - v.shape[0]))[None, :]


def _layer(layer, x2, b, seq, cos_t, sin_t, pre_g, post_g, w_in, w_out, conv_w, conv_b, dt_bias, a_log,
           ssd_d, ssd_norm, sinks, s5_tabs, s5_d, glu_w, glu_b):
    proj = _inproj(x2, pre_g[None, :], w_in, layer, cos_t, sin_t, seq, min(TM_INPROJ, seq))
    proj3 = proj.reshape(b, seq, PROJ_W)

    y_moba = _moba(proj3)
    y_swa = _swa(proj3, sinks)
    y_ssd = _ssd(proj3, conv_w, conv_b[None, :], _pad_lanes(dt_bias), _pad_lanes(a_log),
                 jnp.repeat(ssd_d, SSD_HEAD_DIM)[None, :], ssd_norm[None, :])
    y_s5 = _s5_core(proj3, *s5_tabs, layer).reshape(b * seq, S5_W)

    m = b * seq
    return _outproj(y_moba.reshape(m, MOBA_W), y_ssd.reshape(m, SSD_W), y_swa.reshape(m, SWA_W),
                    y_s5, proj, x2, w_out, post_g[None, :], s5_d[None, :],
                    glu_w, glu_b[None, :], layer, min(TM_OUTPROJ, seq))


def kernel(x, pre_norm, post_norm, w_in, w_out, ssd_conv_w, ssd_conv_b, ssd_dt_bias, ssd_a_log, ssd_d, ssd_norm, swa_sinks, s5_a_re, s5_a_im, s5_log_dt, s5_b_re, s5_b_im, s5_c_re, s5_c_im, s5_d, s5_glu_w, s5_glu_b):
    b, seq, d = x.shape
    assert d == D_MODEL
    assert seq % MOBA_BLOCK == 0 and seq % SSD_CHUNK == 0 and seq % SWA_WINDOW == 0
    assert seq % (S5_CHUNK * S5_SCAN_ROWS) == 0
    assert seq % SWA_TQ == 0
    depth = pre_norm.shape[0]
    cos_t, sin_t = _rope_tables(seq)
    w_cat = _cat_w_in(w_in)
    swa_lo = MOBA_W + SSD_W
    w_out_p = jnp.concatenate([w_out[:, :swa_lo], _swa_tiles(w_out[:, swa_lo:swa_lo + SWA_W], 1),
                               w_out[:, swa_lo + SWA_W:]], axis=1).astype(BF16)
    per_kv = SWA_HEADS // SWA_KV_HEADS
    sinks_p = jnp.swapaxes(swa_sinks.reshape(depth, SWA_KV_HEADS, per_kv), 1, 2).reshape(depth, SWA_HEADS)
    glu_w16 = s5_glu_w.astype(BF16)
    fold = lambda a: a.reshape((depth * S5_GROUPS,) + a.shape[2:])
    tabs = _s5_tables(fold(s5_a_re), fold(s5_a_im), s5_log_dt.reshape(-1), fold(s5_b_re), fold(s5_b_im),
                      fold(s5_c_re), fold(s5_c_im))

    x2 = x.reshape(b * seq, d)
    for l in range(depth):
        x2 = _layer(l, x2, b, seq, cos_t, sin_t, pre_norm[l], post_norm[l], w_cat, w_out_p,
                    ssd_conv_w[l], ssd_conv_b[l], ssd_dt_bias[l], ssd_a_log[l], ssd_d[l], ssd_norm[l],
                    sinks_p[l], tabs, s5_d[l], glu_w16, s5_glu_b[l])
    return x2.reshape(b, seq, d)
```
